```python
import math
import jax, jax.numpy as jnp
from jax import lax
import numpy as np

D_MODEL = 4096
BATCH = 1
SEQ = 16384
DEPTH = 4

CHUNK = 64
H_SSM = D_MODEL // 2
SSM_GROUP = 16
N_SSM_GROUPS = H_SSM // SSM_GROUP
SSM_STATE = 64
H_CONV = D_MODEL // 2
CONV_WIDTH = 3
N_IN = H_SSM + 3 * H_CONV + 2 * D_MODEL
N_GROUPS = 4
EXPERTS_PER_GROUP = 4
N_EXPERTS = N_GROUPS * EXPERTS_PER_GROUP
TOP_K = 2
D_EXPERT = (3 * D_MODEL) // 32
N_MOD = 6
EPS = 1e-6
DT_MIN = 1e-3
DT_MAX = 1e-1

kernel_name = 'hybrid_s5_shortconv_hmoe_adaln'


def rms_norm(x, g):
    x32 = x.astype(jnp.float32)
    y = x32 * lax.rsqrt(jnp.mean(x32 * x32, axis=-1, keepdims=True) + EPS)
    return y.astype(x.dtype) * g


def _complex_affine_combine(left, right):
    a1r, a1i, b1r, b1i = left
    a2r, a2i, b2r, b2i = right
    ar = a2r * a1r - a2i * a1i
    ai = a2r * a1i + a2i * a1r
    br = a2r * b1r - a2i * b1i + b2r
    bi = a2r * b1i + a2i * b1r + b2i
    return (ar, ai, br, bi)


def s5_branch(u, a_re, a_im, log_dt, b_re, b_im, c_re, c_im, d_skip, w_glu):
    bsz, L, _ = u.shape
    dt_out = u.dtype
    f32 = jnp.float32
    u32 = u.astype(f32)
    ug = u32.reshape(bsz, L, N_SSM_GROUPS, SSM_GROUP)
    delta = jnp.exp(log_dt.astype(f32))[:, None]
    ar = a_re.astype(f32)
    ai = a_im.astype(f32)
    mag = jnp.exp(delta * ar)
    ang = delta * ai
    abar_re = mag * jnp.cos(ang)
    abar_im = mag * jnp.sin(ang)
    den = ar * ar + ai * ai
    zr = abar_re - 1.0
    zi = abar_im
    coef_re = (zr * ar + zi * ai) / den
    coef_im = (zi * ar - zr * ai) / den
    bx_re = jnp.einsum('blgc,gpc->blgp', ug, b_re.astype(f32))
    bx_im = jnp.einsum('blgc,gpc->blgp', ug, b_im.astype(f32))
    bu_re = coef_re * bx_re - coef_im * bx_im
    bu_im = coef_re * bx_im + coef_im * bx_re
    a_seq_re = jnp.broadcast_to(abar_re, bu_re.shape)
    a_seq_im = jnp.broadcast_to(abar_im, bu_im.shape)
    _, _, h_re, h_im = lax.associative_scan(
        _complex_affine_combine, (a_seq_re, a_seq_im, bu_re, bu_im), axis=1)
    y = (jnp.einsum('blgp,gcp->blgc', h_re, c_re.astype(f32))
         - jnp.einsum('blgp,gcp->blgc', h_im, c_im.astype(f32)))
    y = y.reshape(bsz, L, H_SSM) + d_skip.astype(f32) * u32
    y = jax.nn.gelu(y).astype(dt_out)
    return y * jax.nn.sigmoid(y @ w_glu)


def short_conv_branch(b_gate, c_gate, v, conv_w):
    z = c_gate * v
    L = z.shape[1]
    zp = jnp.pad(z, ((0, 0), (CONV_WIDTH - 1, 0), (0, 0)))
    conv = conv_w[0] * zp[:, 0:L]
    for k in range(1, CONV_WIDTH):
        conv = conv + conv_w[k] * zp[:, k:k + L]
    return b_gate * conv


def hybrid_mixer(h, w_in, a_re, a_im, log_dt, b_re, b_im, c_re, c_im, d_skip, w_glu,
                 w_br_ssm, conv_w, w_br_conv, w_o):
    proj = h @ w_in
    cuts = [H_SSM, H_SSM + H_CONV, H_SSM + 2 * H_CONV, H_SSM + 3 * H_CONV,
            H_SSM + 3 * H_CONV + D_MODEL]
    u, b_gate, c_gate, v, g_ssm, g_conv = jnp.split(proj, cuts, axis=-1)
    y_ssm = s5_branch(u, a_re, a_im, log_dt, b_re, b_im, c_re, c_im, d_skip, w_glu) @ w_br_ssm
    y_conv = short_conv_branch(b_gate, c_gate, v, conv_w) @ w_br_conv
    merged = jax.nn.sigmoid(g_ssm) * y_ssm + jax.nn.sigmoid(g_conv) * y_conv
    return merged @ w_o


def hierarchical_moe(h, w_rg, b_rg, w_re, b_re, w_g, w_u, w_d):
    bsz, L, d = h.shape
    f32 = jnp.float32
    t = h.reshape(bsz * L, d)
    lg = (t @ w_rg + b_rg).astype(f32)
    p_group = jax.nn.softmax(lg, axis=-1)
    _, gsel = lax.top_k(lg, 1)
    p_sel = jnp.take_along_axis(p_group, gsel, axis=-1)
    le = (t @ w_re + b_re).astype(f32).reshape(-1, N_GROUPS, EXPERTS_PER_GROUP)
    le_sel = jnp.take_along_axis(le, gsel[:, :, None], axis=1)[:, 0]
    top_v, top_i = lax.top_k(le_sel, TOP_K)
    w_k = p_sel * jax.nn.softmax(top_v, axis=-1)
    expert_idx = gsel * EXPERTS_PER_GROUP + top_i
    combine = jnp.sum(jax.nn.one_hot(expert_idx, N_EXPERTS, dtype=f32) * w_k[..., None],
                      axis=1).astype(h.dtype)
    hg = jnp.einsum('td,edf->tef', t, w_g)
    hu = jnp.einsum('td,edf->tef', t, w_u)
    act = jax.nn.silu(hg) * hu * combine[..., None]
    y = jnp.einsum('tef,efd->td', act, w_d)
    return y.reshape(bsz, L, d)


def setup_inputs(seed: int = 0) -> dict:
    key = jax.random.key(seed)
    ks = jax.random.split(key, 32)
    f32 = jnp.float32

    def nrm(k, shape, scale):
        return jax.random.normal(k, shape, f32) * scale

    G, P, GC = N_SSM_GROUPS, SSM_STATE, SSM_GROUP
    a_im_base = math.pi * jnp.arange(P, dtype=f32)
    return {
        'x': nrm(ks[0], (BATCH, SEQ, D_MODEL), 1.0),
        'c': nrm(ks[1], (BATCH, D_MODEL), 1.0),
        'w_mod': nrm(ks[2], (D_MODEL, N_MOD * D_MODEL), 0.2 * D_MODEL ** -0.5),
        'mod_table': nrm(ks[3], (DEPTH, N_MOD * D_MODEL), 0.1),
        'norm1_g': 1.0 + nrm(ks[4], (DEPTH, D_MODEL), 0.02),
        'w_in': nrm(ks[5], (DEPTH, D_MODEL, N_IN), D_MODEL ** -0.5),
        'ssm_a_re': -0.5 + nrm(ks[6], (DEPTH, G, P), 0.01),
        'ssm_a_im': a_im_base + nrm(ks[7], (DEPTH, G, P), 0.01),
        'ssm_log_dt': jax.random.uniform(ks[8], (DEPTH, G), f32,
                                         math.log(DT_MIN), math.log(DT_MAX)),
        'ssm_b_re': nrm(ks[9], (DEPTH, G, P, GC), (2 * GC) ** -0.5),
        'ssm_b_im': nrm(ks[10], (DEPTH, G, P, GC), (2 * GC) ** -0.5),
        'ssm_c_re': nrm(ks[11], (DEPTH, G, GC, P), (2 * P) ** -0.5),
        'ssm_c_im': nrm(ks[12], (DEPTH, G, GC, P), (2 * P) ** -0.5),
        'ssm_d': nrm(ks[13], (DEPTH, H_SSM), 1.0),
        'w_glu': nrm(ks[14], (DEPTH, H_SSM, H_SSM), H_SSM ** -0.5),
        'w_br_ssm': nrm(ks[15], (DEPTH, H_SSM, D_MODEL), H_SSM ** -0.5),
        'conv_w': nrm(ks[16], (DEPTH, CONV_WIDTH, H_CONV), 0.5),
        'w_br_conv': nrm(ks[17], (DEPTH, H_CONV, D_MODEL), H_CONV ** -0.5),
        'w_o': nrm(ks[18], (DEPTH, D_MODEL, D_MODEL), D_MODEL ** -0.5),
        'norm2_g': 1.0 + nrm(ks[19], (DEPTH, D_MODEL), 0.02),
        'w_router_group': nrm(ks[20], (DEPTH, D_MODEL, N_GROUPS), D_MODEL ** -0.5),
        'b_router_group': nrm(ks[21], (DEPTH, N_GROUPS), 0.01),
        'w_router_expert': nrm(ks[22], (DEPTH, D_MODEL, N_EXPERTS), D_MODEL ** -0.5),
        'b_router_expert': nrm(ks[23], (DEPTH, N_EXPERTS), 0.01),
        'w_exp_gate': nrm(ks[24], (DEPTH, N_EXPERTS, D_MODEL, D_EXPERT), D_MODEL ** -0.5),
        'w_exp_up': nrm(ks[25], (DEPTH, N_EXPERTS, D_MODEL, D_EXPERT), D_MODEL ** -0.5),
        'w_exp_down': nrm(ks[26], (DEPTH, N_EXPERTS, D_EXPERT, D_MODEL), D_EXPERT ** -0.5),
        'final_g': 1.0 + nrm(ks[27], (D_MODEL,), 0.02),
    }


def reference(x, c, w_mod, mod_table, norm1_g, w_in, ssm_a_re, ssm_a_im, ssm_log_dt,
              ssm_b_re, ssm_b_im, ssm_c_re, ssm_c_im, ssm_d, w_glu, w_br_ssm, conv_w,
              w_br_conv, w_o, norm2_g, w_router_group, b_router_group, w_router_expert,
              b_router_expert, w_exp_gate, w_exp_up, w_exp_down, final_g):
    mod_shared = jax.nn.silu(c) @ w_mod
    for l in range(DEPTH):
        mod = (mod_shared + mod_table[l])[:, None, :]
        sh1, sc1, g1, sh2, sc2, g2 = jnp.split(mod, N_MOD, axis=-1)
        h = rms_norm(x, norm1_g[l]) * (1.0 + sc1) + sh1
        x = x + g1 * hybrid_mixer(h, w_in[l], ssm_a_re[l], ssm_a_im[l], ssm_log_dt[l],
                                  ssm_b_re[l], ssm_b_im[l], ssm_c_re[l], ssm_c_im[l],
                                  ssm_d[l], w_glu[l], w_br_ssm[l], conv_w[l],
                                  w_br_conv[l], w_o[l])
        h = rms_norm(x, norm2_g[l]) * (1.0 + sc2) + sh2
        x = x + g2 * hierarchical_moe(h, w_router_group[l], b_router_group[l],
                                      w_router_expert[l], b_router_expert[l],
                                      w_exp_gate[l], w_exp_up[l], w_exp_down[l])
    return rms_norm(x, final_g)
```

```python
import functools
import math

import jax
import jax.numpy as jnp
from jax import lax
from jax.experimental import pallas as pl
from jax.experimental.pallas import tpu as pltpu

F32 = jnp.float32
BF16 = jnp.bfloat16

EPS = 1e-6
SSM_GROUP = 16
SSM_STATE = 64
CONV_WIDTH = 3
N_GROUPS = 4
EXPERTS_PER_GROUP = 4
N_EXPERTS = N_GROUPS * EXPERTS_PER_GROUP
N_MOD = 6

LANES = 128
SUBLANES = 8
VMEM_LIMIT_BYTES = 58 * 1024 * 1024

SSM_LC = 16
SSM_LA = 4
SSM_LB = 4
SSM_S2 = 16
SSM_K2 = 16
SSM_NK = SSM_S2 * SSM_K2
SSM_TT = SSM_NK * SSM_LC
SSM_CB = 256
SSM_GB = SSM_CB // SSM_GROUP
SSM_NS = SSM_GB * SSM_STATE
LT_ROWS = 24


def _cparams(sem):
    return pltpu.CompilerParams(dimension_semantics=sem, vmem_limit_bytes=VMEM_LIMIT_BYTES)


def _mod_kernel(c_ref, w_ref, t_ref, o_ref):
    c = c_ref[...]
    s = c * jax.nn.sigmoid(c)
    shared = jnp.sum(s * w_ref[...], axis=0, keepdims=True)
    o_ref[...] = t_ref[...] + shared


def _mod_call(c, w_mod, mod_table):
    d, n = w_mod.shape
    depth = mod_table.shape[0]
    tn = 512
    return pl.pallas_call(
        _mod_kernel,
        grid=(n // tn,),
        in_specs=[pl.BlockSpec((d, 1), lambda j: (0, 0)),
                  pl.BlockSpec((d, tn), lambda j: (0, j)),
                  pl.BlockSpec((depth, tn), lambda j: (0, j))],
        out_specs=pl.BlockSpec((depth, tn), lambda j: (0, j)),
        out_shape=jax.ShapeDtypeStruct((depth, n), F32),
        compiler_params=_cparams(("arbitrary",)),
        name="adaln_mod",
    )(c.reshape(d, 1), w_mod, mod_table)


def _norm_mod_kernel(x_ref, g_ref, sc_ref, sh_ref, o_ref):
    x = x_ref[...]
    y = x * lax.rsqrt(jnp.mean(x * x, axis=-1, keepdims=True) + EPS)
    o_ref[...] = ((y * g_ref[...]) * (1.0 + sc_ref[...]) + sh_ref[...]).astype(o_ref.dtype)


def _norm_kernel(x_ref, g_ref, o_ref):
    x = x_ref[...]
    y = x * lax.rsqrt(jnp.mean(x * x, axis=-1, keepdims=True) + EPS)
    o_ref[...] = (y * g_ref[...]).astype(o_ref.dtype)


def _norm_mod_call(x, g, sc, sh, tm=256):
    t, d = x.shape
    row = pl.BlockSpec((1, d), lambda i: (0, 0))
    return pl.pallas_call(
        _norm_mod_kernel,
        grid=(t // tm,),
        in_specs=[pl.BlockSpec((tm, d), lambda i: (i, 0)), row, row, row],
        out_specs=pl.BlockSpec((tm, d), lambda i: (i, 0)),
        out_shape=jax.ShapeDtypeStruct((t, d), BF16),
        compiler_params=_cparams(("parallel",)),
        name="norm_mod",
    )(x, g, sc, sh)


def _norm_call(x, g, tm=256):
    t, d = x.shape
    return pl.pallas_call(
        _norm_kernel,
        grid=(t // tm,),
        in_specs=[pl.BlockSpec((tm, d), lambda i: (i, 0)), pl.BlockSpec((1, d), lambda i: (0, 0))],
        out_specs=pl.BlockSpec((tm, d), lambda i: (i, 0)),
        out_shape=jax.ShapeDtypeStruct((t, d), F32),
        compiler_params=_cparams(("parallel",)),
        name="final_norm",
    )(x, g)


def _mm_kernel(a_ref, w_ref, o_ref):
    o_ref[...] = jnp.dot(a_ref[...], w_ref[...], preferred_element_type=F32).astype(o_ref.dtype)


def _mm_call(a, w, out_dtype, tm, tn, name):
    t, k = a.shape
    n = w.shape[1]
    return pl.pallas_call(
        _mm_kernel,
        grid=(t // tm, n // tn),
        in_specs=[pl.BlockSpec((tm, k), lambda i, j: (i, 0)),
                  pl.BlockSpec((k, tn), lambda i, j: (0, j))],
        out_specs=pl.BlockSpec((tm, tn), lambda i, j: (i, j)),
        out_shape=jax.ShapeDtypeStruct((t, n), out_dtype),
        compiler_params=_cparams(("parallel", "arbitrary")),
        name=name,
    )(a, w)


def _glu_kernel(a_ref, y_ref, w_ref, o_ref):
    acc = jnp.dot(a_ref[...], w_ref[...], preferred_element_type=F32)
    o_ref[...] = (y_ref[...].astype(F32) * jax.nn.sigmoid(acc)).astype(o_ref.dtype)


def _glu_call(y, w, tm=1024, tn=512):
    t, k = y.shape
    n = w.shape[1]
    return pl.pallas_call(
        _glu_kernel,
        grid=(t // tm, n // tn),
        in_specs=[pl.BlockSpec((tm, k), lambda i, j: (i, 0)),
                  pl.BlockSpec((tm, tn), lambda i, j: (i, j)),
                  pl.BlockSpec((k, tn), lambda i, j: (0, j))],
        out_specs=pl.BlockSpec((tm, tn), lambda i, j: (i, j)),
        out_shape=jax.ShapeDtypeStruct((t, n), BF16),
        compiler_params=_cparams(("parallel", "arbitrary")),
        name="s5_glu",
    )(y, y, w)


def _merge_kernel(ys_ref, yc_ref, ws_ref, wc_ref, gs_ref, gc_ref, o_ref):
    y_ssm = jnp.dot(ys_ref[...], ws_ref[...], preferred_element_type=F32)
    y_conv = jnp.dot(yc_ref[...], wc_ref[...], preferred_element_type=F32)
    merged = jax.nn.sigmoid(gs_ref[...]) * y_ssm + jax.nn.sigmoid(gc_ref[...]) * y_conv
    o_ref[...] = merged.astype(o_ref.dtype)


def _merge_call(ys, yc, ws, wc, proj, gs_col, gc_col, tm=1024, tn=512):
    t, k = ys.shape
    n = ws.shape[1]
    gs_blk, gc_blk = gs_col // tn, gc_col // tn
    return pl.pallas_call(
        _merge_kernel,
        grid=(t // tm, n // tn),
        in_specs=[pl.BlockSpec((tm, k), lambda i, j: (i, 0)),
                  pl.BlockSpec((tm, k), lambda i, j: (i, 0)),
                  pl.BlockSpec((k, tn), lambda i, j: (0, j)),
                  pl.BlockSpec((k, tn), lambda i, j: (0, j)),
                  pl.BlockSpec((tm, tn), lambda i, j: (i, gs_blk + j)),
                  pl.BlockSpec((tm, tn), lambda i, j: (i, gc_blk + j))],
        out_specs=pl.BlockSpec((tm, tn), lambda i, j: (i, j)),
        out_shape=jax.ShapeDtypeStruct((t, n), BF16),
        compiler_params=_cparams(("parallel", "arbitrary")),
        name="branch_merge",
    )(ys, yc, ws, wc, proj, proj)


def _resid_kernel(a_ref, w_ref, x_ref, g_ref, o_ref):
    acc = jnp.dot(a_ref[...], w_ref[...], preferred_element_type=F32)
    o_ref[...] = x_ref[...] + g_ref[...] * acc


def _resid_call(a, w, x, gate, tm, tn, name):
    t, k = a.shape
    n = w.shape[1]
    return pl.pallas_call(
        _resid_kernel,
        grid=(t // tm, n // tn),
        in_specs=[pl.BlockSpec((tm, k), lambda i, j: (i, 0)),
                  pl.BlockSpec((k, tn), lambda i, j: (0, j)),
                  pl.BlockSpec((tm, tn), lambda i, j: (i, j)),
                  pl.BlockSpec((1, tn), lambda i, j: (0, j))],
        out_specs=pl.BlockSpec((tm, tn), lambda i, j: (i, j)),
        out_shape=jax.ShapeDtypeStruct((t, n), F32),
        compiler_params=_cparams(("parallel", "arbitrary")),
        name=name,
    )(a, w, x, gate)


def _conv_kernel(b_ref, c_ref, v_ref, ch_ref, vh_ref, w_ref, o_ref, z_scr):
    tm = b_ref.shape[0]
    halo = SUBLANES
    z = c_ref[...] * v_ref[...]
    zh = jnp.where(pl.program_id(0) == 0, 0.0, ch_ref[...] * vh_ref[...])
    z_scr[0:halo, :] = zh
    z_scr[halo:, :] = z
    w = w_ref[...]
    conv = w[2:3, :] * z
    for k in range(CONV_WIDTH - 1):
        conv = conv + w[k:k + 1, :] * z_scr[pl.ds(halo - (CONV_WIDTH - 1 - k), tm), :]
    o_ref[...] = (b_ref[...] * conv).astype(o_ref.dtype)


def _conv_call(proj, conv_w, b_col, c_col, v_col, width, tm=512, tc=512):
    t = proj.shape[0]
    bb, cb, vb = b_col // tc, c_col // tc, v_col // tc
    hb = tm // SUBLANES

    def halo_map(off):
        return lambda i, j: (jnp.maximum(i * hb - 1, 0), off + j)

    return pl.pallas_call(
        _conv_kernel,
        grid=(t // tm, width // tc),
        in_specs=[pl.BlockSpec((tm, tc), lambda i, j: (i, bb + j)),
                  pl.BlockSpec((tm, tc), lambda i, j: (i, cb + j)),
                  pl.BlockSpec((tm, tc), lambda i, j: (i, vb + j)),
                  pl.BlockSpec((SUBLANES, tc), halo_map(cb)),
                  pl.BlockSpec((SUBLANES, tc), halo_map(vb)),
                  pl.BlockSpec((CONV_WIDTH, tc), lambda i, j: (0, j))],
        out_specs=pl.BlockSpec((tm, tc), lambda i, j: (i, j)),
        out_shape=jax.ShapeDtypeStruct((t, width), BF16),
        scratch_shapes=[pltpu.VMEM((tm + SUBLANES, tc), F32)],
        compiler_params=_cparams(("parallel", "parallel")),
        name="short_conv",
    )(proj, proj, proj, proj, proj, conv_w)


def _cmul_row(xr, xi, lr, li):
    return xr * lr - xi * li, xr * li + xi * lr


def _gelu_tanh(x):
    return 0.5 * x * (1.0 + jnp.tanh(math.sqrt(2.0 / math.pi) * (x + 0.044715 * (x * x * x))))


def _ssm_kernel(u0_ref, u1_ref, kall_ref, p_ref, q_ref, lt_ref, d_ref, o_ref,
                ub2, y2, bx, s_scr, ha, cs, carry, yo0, yo1):
    ns = SSM_NS
    cb = SSM_CB
    rows = SSM_K2
    kstride = SSM_S2 * SSM_LC
    u_halves = (u0_ref, u1_ref)
    yo = (yo0, yo1)

    @pl.when(pl.program_id(1) == 0)
    def _():
        carry[...] = jnp.zeros_like(carry)

    def lt(row):
        return lt_ref[row:row + 1, 0:ns], lt_ref[row:row + 1, ns:2 * ns]

    for s in range(SSM_LC):
        for s2 in range(SSM_S2):
            idx = pl.ds(s2 * SSM_LC + s, rows, stride=kstride)
            for hf, u_ref in enumerate(u_halves):
                c0 = s * cb + hf * LANES
                ub2[s2 * rows:(s2 + 1) * rows, c0:c0 + LANES] = u_ref[idx, :].astype(BF16)

    for sp in range(SSM_LC):
        y2[:, sp * cb:(sp + 1) * cb] = jnp.dot(
            ub2[:, 0:(sp + 1) * cb], kall_ref[(SSM_LC - 1 - sp) * cb:, :],
            preferred_element_type=F32)

    s_scr[...] = jnp.dot(ub2[:, (SSM_LA - 1) * SSM_LB * cb:], p_ref[...], preferred_element_type=F32)
    for a in range(SSM_LA - 1):
        bx[...] = jnp.dot(ub2[:, a * SSM_LB * cb:(a + 1) * SSM_LB * cb], p_ref[...],
                          preferred_element_type=F32)
        lr, li = lt(SSM_LA - 2 - a)
        for s2 in range(SSM_S2):
            sl = slice(s2 * rows, (s2 + 1) * rows)
            pr, pi = _cmul_row(bx[sl, 0:ns], bx[sl, ns:], lr, li)
            s_scr[sl, 0:ns] += pr
            s_scr[sl, ns:] += pi

    lr, li = lt(4 + 1)
    for s2 in range(1, SSM_S2):
        pv = slice((s2 - 1) * rows, s2 * rows)
        sl = slice(s2 * rows, (s2 + 1) * rows)
        pr, pi = _cmul_row(s_scr[pv, 0:ns], s_scr[pv, ns:], lr, li)
        s_scr[sl, 0:ns] += pr
        s_scr[sl, ns:] += pi

    lr, li = lt(3)
    last = slice((SSM_S2 - 1) * rows, SSM_S2 * rows)
    zr, zi = s_scr[last, 0:ns], s_scr[last, ns:]
    cr, ci = carry[0:1, 0:ns], carry[0:1, ns:]
    for k2 in range(SSM_K2):
        cs[k2:k2 + 1, 0:ns] = cr
        cs[k2:k2 + 1, ns:] = ci
        pr, pi = _cmul_row(cr, ci, lr, li)
        cr, ci = pr + zr[k2:k2 + 1, :], pi + zi[k2:k2 + 1, :]
    carry[0:1, 0:ns] = cr
    carry[0:1, ns:] = ci

    for a in range(SSM_LA):
        for s2 in range(SSM_S2):
            sl = slice(s2 * rows, (s2 + 1) * rows)
            hr, hi = cs[:, 0:ns], cs[:, ns:]
            if s2 > 0:
                pv = slice((s2 - 1) * rows, s2 * rows)
                lr, li = lt(4 + s2)
                pr, pi = _cmul_row(hr, hi, lr, li)
                hr, hi = s_scr[pv, 0:ns] + pr, s_scr[pv, ns:] + pi
            if a > 0:
                lr, li = lt(a - 1)
                hr, hi = _cmul_row(hr, hi, lr, li)
            ha[sl, 0:ns] = hr.astype(BF16)
            ha[sl, ns:] = hi.astype(BF16)
        wa = SSM_LB * cb
        y2[:, a * wa:(a + 1) * wa] += jnp.dot(ha[...], q_ref[...], preferred_element_type=F32)

    d = d_ref[...]
    for s in range(SSM_LC):
        for s2 in range(SSM_S2):
            idx = pl.ds(s2 * SSM_LC + s, rows, stride=kstride)
            for hf, u_ref in enumerate(u_halves):
                c0 = s * cb + hf * LANES
                y = (y2[s2 * rows:(s2 + 1) * rows, c0:c0 + LANES]
                     + d[:, hf * LANES:(hf + 1) * LANES] * u_ref[idx, :])
                yo[hf][idx, :] = _gelu_tanh(y)
    for hf in range(len(u_halves)):
        o_ref[:, hf * LANES:(hf + 1) * LANES] = yo[hf][...].astype(o_ref.dtype)


def _ssm_call(proj, u_col, h_ssm, kall, pstack, qcat, ltab, d_skip):
    t = proj.shape[0]
    nb = h_ssm // SSM_CB
    ub = u_col // SSM_CB
    ns2 = 2 * SSM_NS
    return pl.pallas_call(
        _ssm_kernel,
        grid=(nb, t // SSM_TT),
        in_specs=[pl.BlockSpec((SSM_TT, LANES), lambda b, i: (i, 2 * (ub + b))),
                  pl.BlockSpec((SSM_TT, LANES), lambda b, i: (i, 2 * (ub + b) + 1)),
                  pl.BlockSpec((None, SSM_LC * SSM_CB, SSM_CB), lambda b, i: (b, 0, 0)),
                  pl.BlockSpec((None, SSM_LB * SSM_CB, ns2), lambda b, i: (b, 0, 0)),
                  pl.BlockSpec((None, ns2, SSM_LB * SSM_CB), lambda b, i: (b, 0, 0)),
                  pl.BlockSpec((None, LT_ROWS, ns2), lambda b, i: (b, 0, 0)),
                  pl.BlockSpec((1, SSM_CB), lambda b, i: (0, b))],
        out_specs=pl.BlockSpec((SSM_TT, SSM_CB), lambda b, i: (i, b)),
        out_shape=jax.ShapeDtypeStruct((t, h_ssm), BF16),
        scratch_shapes=[pltpu.VMEM((SSM_NK, SSM_LC * SSM_CB), BF16),
                        pltpu.VMEM((SSM_NK, SSM_LC * SSM_CB), F32),
                        pltpu.VMEM((SSM_NK, ns2), F32),
                        pltpu.VMEM((SSM_NK, ns2), F32),
                        pltpu.VMEM((SSM_NK, ns2), BF16),
                        pltpu.VMEM((SSM_K2, ns2), F32),
                        pltpu.VMEM((SUBLANES, ns2), F32),
                        pltpu.VMEM((SSM_TT, LANES), F32),
                        pltpu.VMEM((SSM_TT, LANES), F32)],
        compiler_params=_cparams(("arbitrary", "arbitrary")),
        name="s5_scan",
    )(proj, proj, kall, pstack, qcat, ltab, d_skip)


def _block_diag(m, nb):
    g, a, b = m.shape
    gb = g // nb
    m = m.reshape(nb, gb, a, b)
    eye = jnp.eye(gb, dtype=m.dtype)
    out = m[:, :, :, None, :] * eye[None, :, None, :, None]
    return out.reshape(nb, gb * a, gb * b)


def _ssm_tables(a_re, a_im, log_dt, b_re, b_im, c_re, c_im):
    g = a_re.shape[0]
    nb = g // SSM_GB
    hi = lax.Precision.HIGHEST
    delta = jnp.exp(log_dt)[:, None]
    mag = jnp.exp(delta * a_re)
    ang = delta * a_im
    lam = (mag * jnp.cos(ang), mag * jnp.sin(ang))
    den = a_re * a_re + a_im * a_im
    zr, zi = lam[0] - 1.0, lam[1]
    coef_re = (zr * a_re + zi * a_im) / den
    coef_im = (zi * a_re - zr * a_im) / den
    bt_re = coef_re[..., None] * b_re - coef_im[..., None] * b_im
    bt_im = coef_re[..., None] * b_im + coef_im[..., None] * b_re

    def cmul(x, y):
        return x[0] * y[0] - x[1] * y[1], x[0] * y[1] + x[1] * y[0]

    def powers(base, n):
        out = [(jnp.ones_like(base[0]), jnp.zeros_like(base[0]))]
        for _ in range(n):
            out.append(cmul(out[-1], base))
        return out

    pw = powers(lam, SSM_LC)
    pw16 = powers(pw[SSM_LC], SSM_S2)

    ks = []
    for tau in range(SSM_LC):
        wr = pw[tau][0][..., None] * bt_re - pw[tau][1][..., None] * bt_im
        wi = pw[tau][0][..., None] * bt_im + pw[tau][1][..., None] * bt_re
        k = (jnp.einsum("gop,gpi->gio", c_re, wr, precision=hi)
             - jnp.einsum("gop,gpi->gio", c_im, wi, precision=hi))
        ks.append(_block_diag(k, nb))
    kall = jnp.concatenate(ks[::-1], axis=1).astype(BF16)

    ps = []
    for b in range(SSM_LB):
        w = pw[SSM_LB - 1 - b]
        wr = w[0][..., None] * bt_re - w[1][..., None] * bt_im
        wi = w[0][..., None] * bt_im + w[1][..., None] * bt_re
        ps.append(jnp.concatenate([_block_diag(wr.transpose(0, 2, 1), nb),
                                   _block_diag(wi.transpose(0, 2, 1), nb)], axis=2))
    pstack = jnp.concatenate(ps, axis=1).astype(BF16)

    qs = []
    for b in range(SSM_LB):
        w = pw[b + 1]
        dr = c_re * w[0][:, None, :] - c_im * w[1][:, None, :]
        di = c_re * w[1][:, None, :] + c_im * w[0][:, None, :]
        qs.append(jnp.concatenate([_block_diag(dr.transpose(0, 2, 1), nb),
                                   _block_diag(-di.transpose(0, 2, 1), nb)], axis=1))
    qcat = jnp.concatenate(qs, axis=2).astype(BF16)

    def row(w):
        return jnp.concatenate([w[0].reshape(nb, 1, SSM_NS), w[1].reshape(nb, 1, SSM_NS)], axis=2)

    rows = [row(pw[SSM_LB * a]) for a in range(1, SSM_LA)]
    rows.append(row(pw16[SSM_S2]))
    rows += [row(pw16[j]) for j in range(SSM_S2)]
    rows += [jnp.zeros_like(rows[0])] * (LT_ROWS - len(rows))
    ltab = jnp.concatenate(rows, axis=1)
    return kall, pstack, qcat, ltab


def _router_kernel(a_ref, w_ref, b_ref, o_ref):
    lg = jnp.dot(a_ref[...], w_ref[...], preferred_element_type=F32) + b_ref[...]
    col = lax.broadcasted_iota(jnp.int32, lg.shape, 1)
    big = jnp.int32(LANES)
    neg = -jnp.inf
    is_g = col < N_GROUPS
    lgg = jnp.where(is_g, lg, neg)
    gmax = jnp.max(lgg, axis=1, keepdims=True)
    gsel = jnp.min(jnp.where(is_g & (lgg == gmax), col, big), axis=1, keepdims=True)
    denom = jnp.sum(jnp.where(is_g, jnp.exp(lgg - gmax), 0.0), axis=1, keepdims=True)
    p_sel = 1.0 / denom
    ecol = col - N_GROUPS
    egrp = lax.shift_right_arithmetic(ecol, jnp.int32(int(math.log2(EXPERTS_PER_GROUP))))
    in_grp = (ecol >= 0) & (ecol < N_EXPERTS) & (egrp == gsel)
    le = jnp.where(in_grp, lg, neg)
    m1 = jnp.max(le, axis=1, keepdims=True)
    i1 = jnp.min(jnp.where(in_grp & (le == m1), col, big), axis=1, keepdims=True)
    rest = in_grp & (col != i1)
    le2 = jnp.where(rest, lg, neg)
    m2 = jnp.max(le2, axis=1, keepdims=True)
    i2 = jnp.min(jnp.where(rest & (le2 == m2), col, big), axis=1, keepdims=True)
    e2 = jnp.exp(m2 - m1)
    w1 = 1.0 / (1.0 + e2)
    w2 = e2 / (1.0 + e2)
    o_ref[...] = jnp.where(col == i1, p_sel * w1, jnp.where(col == i2, p_sel * w2, 0.0))


def _router_call(h, w_r, b_r, tm=1024):
    t, d = h.shape
    return pl.pallas_call(
        _router_kernel,
        grid=(t // tm,),
        in_specs=[pl.BlockSpec((tm, d), lambda i: (i, 0)),
                  pl.BlockSpec((d, LANES), lambda i: (0, 0)),
                  pl.BlockSpec((1, LANES), lambda i: (0, 0))],
        out_specs=pl.BlockSpec((tm, LANES), lambda i: (i, 0)),
        out_shape=jax.ShapeDtypeStruct((t, LANES), F32),
        compiler_params=_cparams(("parallel",)),
        name="moe_router",
    )(h, w_r, b_r)


def _moe_up_kernel(a_ref, wg_ref, wu_ref, comb_ref, o_ref, *, d_expert, experts_per_tile):
    a = a_ref[...]
    hg = jnp.dot(a, wg_ref[...], preferred_element_type=F32)
    hu = jnp.dot(a, wu_ref[...], preferred_element_type=F32)
    act = (hg * jax.nn.sigmoid(hg)) * hu
    comb = comb_ref[...]
    col = lax.broadcasted_iota(jnp.int32, comb.shape, 1)
    j = pl.program_id(1)
    for e in range(experts_per_tile):
        want = N_GROUPS + j * experts_per_tile + e
        ce = jnp.sum(jnp.where(col == want, comb, 0.0), axis=1, keepdims=True)
        sl = slice(e * d_expert, (e + 1) * d_expert)
        o_ref[:, sl] = (act[:, sl] * ce).astype(o_ref.dtype)


def _moe_up_call(h, wg, wu, comb, d_expert, tm=512, experts_per_tile=2):
    t, d = h.shape
    n = wg.shape[1]
    tn = d_expert * experts_per_tile
    return pl.pallas_call(
        functools.partial(_moe_up_kernel, d_expert=d_expert, experts_per_tile=experts_per_tile),
        grid=(t // tm, n // tn),
        in_specs=[pl.BlockSpec((tm, d), lambda i, j: (i, 0)),
                  pl.BlockSpec((d, tn), lambda i, j: (0, j)),
                  pl.BlockSpec((d, tn), lambda i, j: (0, j)),
                  pl.BlockSpec((tm, LANES), lambda i, j: (i, 0))],
        out_specs=pl.BlockSpec((tm, tn), lambda i, j: (i, j)),
        out_shape=jax.ShapeDtypeStruct((t, n), BF16),
        compiler_params=_cparams(("parallel", "arbitrary")),
        name="moe_up",
    )(h, wg, wu, comb)


def kernel(x, c, w_mod, mod_table, norm1_g, w_in, ssm_a_re, ssm_a_im, ssm_log_dt, ssm_b_re, ssm_b_im, ssm_c_re, ssm_c_im, ssm_d, w_glu, w_br_ssm, conv_w, w_br_conv, w_o, norm2_g, w_router_group, b_router_group, w_router_expert, b_router_expert, w_exp_gate, w_exp_up, w_exp_down, final_g):
    bsz, seq, d = x.shape
    depth = mod_table.shape[0]
    h_ssm = ssm_d.shape[1]
    h_conv = conv_w.shape[2]
    n_exp, _, d_expert = w_exp_gate.shape[1:]
    assert bsz == 1 and seq % SSM_TT == 0 and h_ssm % SSM_CB == 0
    assert ssm_a_re.shape[1:] == (h_ssm // SSM_GROUP, SSM_STATE) and n_exp == N_EXPERTS
    u_col, b_col, c_col, v_col = 0, h_ssm, h_ssm + h_conv, h_ssm + 2 * h_conv
    gs_col = h_ssm + 3 * h_conv
    gc_col = gs_col + d

    xt = x.reshape(seq, d)
    mod = _mod_call(c, w_mod, mod_table)

    for l in range(depth):
        m = [mod[l:l + 1, k * d:(k + 1) * d] for k in range(N_MOD)]
        sh1, sc1, g1, sh2, sc2, g2 = m

        h1 = _norm_mod_call(xt, norm1_g[l:l + 1], sc1, sh1)
        proj = _mm_call(h1, w_in[l].astype(BF16), F32, 1024, 512, "in_proj")
        kall, pstack, qcat, ltab = _ssm_tables(ssm_a_re[l], ssm_a_im[l], ssm_log_dt[l], ssm_b_re[l],
                                               ssm_b_im[l], ssm_c_re[l], ssm_c_im[l])
        y_pre = _ssm_call(proj, u_col, h_ssm, kall, pstack, qcat, ltab, ssm_d[l:l + 1])
        ys = _glu_call(y_pre, w_glu[l].astype(BF16))
        yc = _conv_call(proj, conv_w[l], b_col, c_col, v_col, h_conv)
        merged = _merge_call(ys, yc, w_br_ssm[l].astype(BF16), w_br_conv[l].astype(BF16),
                             proj, gs_col, gc_col)
        xt = _resid_call(merged, w_o[l].astype(BF16), xt, g1, 1024, 512, "out_proj")

        h2 = _norm_mod_call(xt, norm2_g[l:l + 1], sc2, sh2)
        pad = LANES - N_GROUPS - N_EXPERTS
        w_r = jnp.concatenate([w_router_group[l], w_router_expert[l], jnp.zeros((d, pad), F32)], axis=1)
        b_r = jnp.concatenate([b_router_group[l], b_router_expert[l], jnp.zeros((pad,), F32)])[None, :]
        comb = _router_call(h2, w_r.astype(BF16), b_r)
        wg = w_exp_gate[l].astype(BF16).transpose(1, 0, 2).reshape(d, n_exp * d_expert)
        wu = w_exp_up[l].astype(BF16).transpose(1, 0, 2).reshape(d, n_exp * d_expert)
        wd = w_exp_down[l].astype(BF16).reshape(n_exp * d_expert, d)
        act = _moe_up_call(h2, wg, wu, comb, d_expert)
        xt = _resid_call(act, wd, xt, g2, 1024, 512, "moe_down")

    out = _norm_call(xt, final_g[None, :])
    return out.reshape(bsz, seq, d)
```

```python
import functools
import math

import jax
import jax.numpy as jnp
from jax import lax
from jax.experimental import pallas as pl
from jax.experimental.pallas import tpu as pltpu

F32 = jnp.float32
BF16 = jnp.bfloat16
I32 = jnp.int32

EPS = 1e-6
SSM_GROUP = 16
SSM_STATE = 64
CONV_WIDTH = 3
N_GROUPS = 4
EXPERTS_PER_GROUP = 4
N_EXPERTS = N_GROUPS * EXPERTS_PER_GROUP
N_MOD = 6

LANES = 128
SUBLANES = 8
VMEM_LIMIT_BYTES = 58 * 1024 * 1024

SSM_LC = 16
SSM_LA = 4
SSM_LB = 4
SSM_S2 = 16
SSM_K2 = 16
SSM_NK = SSM_S2 * SSM_K2
SSM_TT = SSM_NK * SSM_LC
SSM_CB = 256
SSM_GB = SSM_CB // SSM_GROUP
SSM_NS = SSM_GB * SSM_STATE
LT_ROWS = 24
SSM_NPOW = 32

MOE_TM = 512
BAND = 2 * SUBLANES
MOE_MAXR = 2 * MOE_TM + N_EXPERTS * BAND
MOE_MAXB = MOE_MAXR // BAND
MOE_TMG = 512
MOE_COLS = 1024


def _cparams(sem):
    return pltpu.CompilerParams(dimension_semantics=sem, vmem_limit_bytes=VMEM_LIMIT_BYTES)


def _mod_kernel(c_ref, w_ref, t_ref, o_ref):
    c = c_ref[...]
    s = c * jax.nn.sigmoid(c)
    shared = jnp.sum(s * w_ref[...], axis=0, keepdims=True)
    o_ref[...] = t_ref[...] + shared


def _mod_call(c, w_mod, mod_table):
    d, n = w_mod.shape
    depth = mod_table.shape[0]
    tn = 512
    return pl.pallas_call(
        _mod_kernel,
        grid=(n // tn,),
        in_specs=[pl.BlockSpec((d, 1), lambda j: (0, 0)),
                  pl.BlockSpec((d, tn), lambda j: (0, j)),
                  pl.BlockSpec((depth, tn), lambda j: (0, j))],
        out_specs=pl.BlockSpec((depth, tn), lambda j: (0, j)),
        out_shape=jax.ShapeDtypeStruct((depth, n), F32),
        compiler_params=_cparams(("arbitrary",)),
        name="adaln_mod",
    )(c.reshape(d, 1), w_mod, mod_table)


def _norm_mod_kernel(x_ref, g_ref, sc_ref, sh_ref, o_ref):
    x = x_ref[...]
    y = x * lax.rsqrt(jnp.mean(x * x, axis=-1, keepdims=True) + EPS)
    o_ref[...] = ((y * g_ref[...]) * (1.0 + sc_ref[...]) + sh_ref[...]).astype(o_ref.dtype)


def _norm_kernel(x_ref, g_ref, o_ref):
    x = x_ref[...]
    y = x * lax.rsqrt(jnp.mean(x * x, axis=-1, keepdims=True) + EPS)
    o_ref[...] = (y * g_ref[...]).astype(o_ref.dtype)


def _norm_mod_call(x, gains, mod, l, sc_blk, sh_blk, tm=256):
    t, d = x.shape
    return pl.pallas_call(
        _norm_mod_kernel,
        grid=(t // tm,),
        in_specs=[pl.BlockSpec((tm, d), lambda i: (i, 0)),
                  pl.BlockSpec((None, 1, d), lambda i: (l, 0, 0)),
                  pl.BlockSpec((None, 1, d), lambda i: (l, 0, sc_blk)),
                  pl.BlockSpec((None, 1, d), lambda i: (l, 0, sh_blk))],
        out_specs=pl.BlockSpec((tm, d), lambda i: (i, 0)),
        out_shape=jax.ShapeDtypeStruct((t, d), BF16),
        compiler_params=_cparams(("parallel",)),
        name="norm_mod",
    )(x, gains, mod, mod)


def _norm_call(x, g, tm=256):
    t, d = x.shape
    return pl.pallas_call(
        _norm_kernel,
        grid=(t // tm,),
        in_specs=[pl.BlockSpec((tm, d), lambda i: (i, 0)), pl.BlockSpec((1, d), lambda i: (0, 0))],
        out_specs=pl.BlockSpec((tm, d), lambda i: (i, 0)),
        out_shape=jax.ShapeDtypeStruct((t, d), F32),
        compiler_params=_cparams(("parallel",)),
        name="final_norm",
    )(x, g)


def _wspec(l, k, tn):
    return pl.BlockSpec((None, k, tn), lambda i, j: (l, 0, j))


def _mm_kernel(a_ref, w_ref, o_ref):
    o_ref[...] = jnp.dot(a_ref[...], w_ref[...], preferred_element_type=F32).astype(o_ref.dtype)


def _mm_call(a, w, l, out_dtype, tm, tn, name):
    t, k = a.shape
    n = w.shape[2]
    return pl.pallas_call(
        _mm_kernel,
        grid=(t // tm, n // tn),
        in_specs=[pl.BlockSpec((tm, k), lambda i, j: (i, 0)), _wspec(l, k, tn)],
        out_specs=pl.BlockSpec((tm, tn), lambda i, j: (i, j)),
        out_shape=jax.ShapeDtypeStruct((t, n), out_dtype),
        compiler_params=_cparams(("parallel", "arbitrary")),
        name=name,
    )(a, w)


def _glu_kernel(a_ref, y_ref, w_ref, o_ref):
    acc = jnp.dot(a_ref[...], w_ref[...], preferred_element_type=F32)
    o_ref[...] = (y_ref[...].astype(F32) * jax.nn.sigmoid(acc)).astype(o_ref.dtype)


def _glu_call(y, w, l, tm=1024, tn=512):
    t, k = y.shape
    n = w.shape[2]
    return pl.pallas_call(
        _glu_kernel,
        grid=(t // tm, n // tn),
        in_specs=[pl.BlockSpec((tm, k), lambda i, j: (i, 0)),
                  pl.BlockSpec((tm, tn), lambda i, j: (i, j)),
                  _wspec(l, k, tn)],
        out_specs=pl.BlockSpec((tm, tn), lambda i, j: (i, j)),
        out_shape=jax.ShapeDtypeStruct((t, n), BF16),
        compiler_params=_cparams(("parallel", "arbitrary")),
        name="s5_glu",
    )(y, y, w)


def _merge_kernel(ys_ref, yc_ref, ws_ref, wc_ref, gs_ref, gc_ref, o_ref):
    y_ssm = jnp.dot(ys_ref[...], ws_ref[...], preferred_element_type=F32)
    y_conv = jnp.dot(yc_ref[...], wc_ref[...], preferred_element_type=F32)
    merged = jax.nn.sigmoid(gs_ref[...]) * y_ssm + jax.nn.sigmoid(gc_ref[...]) * y_conv
    o_ref[...] = merged.astype(o_ref.dtype)


def _merge_call(ys, yc, ws, wc, l, proj, gs_col, gc_col, tm=1024, tn=512):
    t, k = ys.shape
    n = ws.shape[2]
    gs_blk, gc_blk = gs_col // tn, gc_col // tn
    return pl.pallas_call(
        _merge_kernel,
        grid=(t // tm, n // tn),
        in_specs=[pl.BlockSpec((tm, k), lambda i, j: (i, 0)),
                  pl.BlockSpec((tm, k), lambda i, j: (i, 0)),
                  _wspec(l, k, tn),
                  _wspec(l, k, tn),
                  pl.BlockSpec((tm, tn), lambda i, j: (i, gs_blk + j)),
                  pl.BlockSpec((tm, tn), lambda i, j: (i, gc_blk + j))],
        out_specs=pl.BlockSpec((tm, tn), lambda i, j: (i, j)),
        out_shape=jax.ShapeDtypeStruct((t, n), BF16),
        compiler_params=_cparams(("parallel", "arbitrary")),
        name="branch_merge",
    )(ys, yc, ws, wc, proj, proj)


def _resid_kernel(a_ref, w_ref, x_ref, g_ref, o_ref):
    acc = jnp.dot(a_ref[...], w_ref[...], preferred_element_type=F32)
    o_ref[...] = x_ref[...] + g_ref[...] * acc


def _resid_call(a, w, l, x, mod, gate_blk, tm, tn, name):
    t, k = a.shape
    n = w.shape[2]
    gb = gate_blk * (n // tn)
    return pl.pallas_call(
        _resid_kernel,
        grid=(t // tm, n // tn),
        in_specs=[pl.BlockSpec((tm, k), lambda i, j: (i, 0)),
                  _wspec(l, k, tn),
                  pl.BlockSpec((tm, tn), lambda i, j: (i, j)),
                  pl.BlockSpec((None, 1, tn), lambda i, j: (l, 0, gb + j))],
        out_specs=pl.BlockSpec((tm, tn), lambda i, j: (i, j)),
        out_shape=jax.ShapeDtypeStruct((t, n), F32),
        compiler_params=_cparams(("parallel", "arbitrary")),
        name=name,
    )(a, w, x, mod)


def _conv_kernel(b_ref, c_ref, v_ref, ch_ref, vh_ref, w_ref, o_ref, z_scr):
    tm = b_ref.shape[0]
    halo = SUBLANES
    z = c_ref[...] * v_ref[...]
    zh = jnp.where(pl.program_id(0) == 0, 0.0, ch_ref[...] * vh_ref[...])
    z_scr[0:halo, :] = zh
    z_scr[halo:, :] = z
    w = w_ref[...]
    conv = w[2:3, :] * z
    for k in range(CONV_WIDTH - 1):
        conv = conv + w[k:k + 1, :] * z_scr[pl.ds(halo - (CONV_WIDTH - 1 - k), tm), :]
    o_ref[...] = (b_ref[...] * conv).astype(o_ref.dtype)


def _conv_call(proj, conv_w, l, b_col, c_col, v_col, width, tm=512, tc=512):
    t = proj.shape[0]
    bb, cb, vb = b_col // tc, c_col // tc, v_col // tc
    hb = tm // SUBLANES

    def halo_map(off):
        return lambda i, j: (jnp.maximum(i * hb - 1, 0), off + j)

    return pl.pallas_call(
        _conv_kernel,
        grid=(t // tm, width // tc),
        in_specs=[pl.BlockSpec((tm, tc), lambda i, j: (i, bb + j)),
                  pl.BlockSpec((tm, tc), lambda i, j: (i, cb + j)),
                  pl.BlockSpec((tm, tc), lambda i, j: (i, vb + j)),
                  pl.BlockSpec((SUBLANES, tc), halo_map(cb)),
                  pl.BlockSpec((SUBLANES, tc), halo_map(vb)),
                  pl.BlockSpec((None, CONV_WIDTH, tc), lambda i, j: (l, 0, j))],
        out_specs=pl.BlockSpec((tm, tc), lambda i, j: (i, j)),
        out_shape=jax.ShapeDtypeStruct((t, width), BF16),
        scratch_shapes=[pltpu.VMEM((tm + SUBLANES, tc), F32)],
        compiler_params=_cparams(("parallel", "parallel")),
        name="short_conv",
    )(proj, proj, proj, proj, proj, conv_w)


def _cmul_row(xr, xi, lr, li):
    return xr * lr - xi * li, xr * li + xi * lr


def _gelu_tanh(x):
    return 0.5 * x * (1.0 + jnp.tanh(math.sqrt(2.0 / math.pi) * (x + 0.044715 * (x * x * x))))


def _ssm_kernel(u0_ref, u1_ref, kall_ref, p_ref, q_ref, lt_ref, d_ref, o_ref,
                ub2, y2, bx, s_scr, ha, cs, carry, yo0, yo1):
    ns = SSM_NS
    cb = SSM_CB
    rows = SSM_K2
    kstride = SSM_S2 * SSM_LC
    u_halves = (u0_ref, u1_ref)
    yo = (yo0, yo1)

    @pl.when(pl.program_id(1) == 0)
    def _():
        carry[...] = jnp.zeros_like(carry)

    def lt(row):
        return lt_ref[row:row + 1, 0:ns], lt_ref[row:row + 1, ns:2 * ns]

    for s in range(SSM_LC):
        for s2 in range(SSM_S2):
            idx = pl.ds(s2 * SSM_LC + s, rows, stride=kstride)
            for hf, u_ref in enumerate(u_halves):
                c0 = s * cb + hf * LANES
                ub2[s2 * rows:(s2 + 1) * rows, c0:c0 + LANES] = u_ref[idx, :].astype(BF16)

    for sp in range(SSM_LC):
        y2[:, sp * cb:(sp + 1) * cb] = jnp.dot(
            ub2[:, 0:(sp + 1) * cb], kall_ref[(SSM_LC - 1 - sp) * cb:, :],
            preferred_element_type=F32)

    s_scr[...] = jnp.dot(ub2[:, (SSM_LA - 1) * SSM_LB * cb:], p_ref[...], preferred_element_type=F32)
    for a in range(SSM_LA - 1):
        bx[...] = jnp.dot(ub2[:, a * SSM_LB * cb:(a + 1) * SSM_LB * cb], p_ref[...],
                          preferred_element_type=F32)
        lr, li = lt(SSM_LA - 2 - a)
        for s2 in range(SSM_S2):
            sl = slice(s2 * rows, (s2 + 1) * rows)
            pr, pi = _cmul_row(bx[sl, 0:ns], bx[sl, ns:], lr, li)
            s_scr[sl, 0:ns] += pr
            s_scr[sl, ns:] += pi

    lr, li = lt(4 + 1)
    for s2 in range(1, SSM_S2):
        pv = slice((s2 - 1) * rows, s2 * rows)
        sl = slice(s2 * rows, (s2 + 1) * rows)
        pr, pi = _cmul_row(s_scr[pv, 0:ns], s_scr[pv, ns:], lr, li)
        s_scr[sl, 0:ns] += pr
        s_scr[sl, ns:] += pi

    lr, li = lt(3)
    last = slice((SSM_S2 - 1) * rows, SSM_S2 * rows)
    zr, zi = s_scr[last, 0:ns], s_scr[last, ns:]
    cr, ci = carry[0:1, 0:ns], carry[0:1, ns:]
    for k2 in range(SSM_K2):
        cs[k2:k2 + 1, 0:ns] = cr
        cs[k2:k2 + 1, ns:] = ci
        pr, pi = _cmul_row(cr, ci, lr, li)
        cr, ci = pr + zr[k2:k2 + 1, :], pi + zi[k2:k2 + 1, :]
    carry[0:1, 0:ns] = cr
    carry[0:1, ns:] = ci

    for a in range(SSM_LA):
        for s2 in range(SSM_S2):
            sl = slice(s2 * rows, (s2 + 1) * rows)
            hr, hi = cs[:, 0:ns], cs[:, ns:]
            if s2 > 0:
                pv = slice((s2 - 1) * rows, s2 * rows)
                lr, li = lt(4 + s2)
                pr, pi = _cmul_row(hr, hi, lr, li)
                hr, hi = s_scr[pv, 0:ns] + pr, s_scr[pv, ns:] + pi
            if a > 0:
                lr, li = lt(a - 1)
                hr, hi = _cmul_row(hr, hi, lr, li)
            ha[sl, 0:ns] = hr.astype(BF16)
            ha[sl, ns:] = hi.astype(BF16)
        wa = SSM_LB * cb
        y2[:, a * wa:(a + 1) * wa] += jnp.dot(ha[...], q_ref[...], preferred_element_type=F32)

    d = d_ref[...]
    for s in range(SSM_LC):
        for s2 in range(SSM_S2):
            idx = pl.ds(s2 * SSM_LC + s, rows, stride=kstride)
            for hf, u_ref in enumerate(u_halves):
                c0 = s * cb + hf * LANES
                y = (y2[s2 * rows:(s2 + 1) * rows, c0:c0 + LANES]
                     + d[:, hf * LANES:(hf + 1) * LANES] * u_ref[idx, :])
                yo[hf][idx, :] = _gelu_tanh(y)
    for hf in range(len(u_halves)):
        o_ref[:, hf * LANES:(hf + 1) * LANES] = yo[hf][...].astype(o_ref.dtype)


def _ssm_call(proj, u_col, h_ssm, kall, pstack, qcat, ltab, d_skip, l):
    t = proj.shape[0]
    nb = h_ssm // SSM_CB
    ub = u_col // SSM_CB
    ns2 = 2 * SSM_NS

    def tab(r, c):
        return pl.BlockSpec((None, None, r, c), lambda b, i: (l, b, 0, 0))

    return pl.pallas_call(
        _ssm_kernel,
        grid=(nb, t // SSM_TT),
        in_specs=[pl.BlockSpec((SSM_TT, LANES), lambda b, i: (i, 2 * (ub + b))),
                  pl.BlockSpec((SSM_TT, LANES), lambda b, i: (i, 2 * (ub + b) + 1)),
                  tab(SSM_LC * SSM_CB, SSM_CB),
                  tab(SSM_LB * SSM_CB, ns2),
                  tab(ns2, SSM_LB * SSM_CB),
                  tab(LT_ROWS, ns2),
                  pl.BlockSpec((None, 1, SSM_CB), lambda b, i: (l, 0, b))],
        out_specs=pl.BlockSpec((SSM_TT, SSM_CB), lambda b, i: (i, b)),
        out_shape=jax.ShapeDtypeStruct((t, h_ssm), BF16),
        scratch_shapes=[pltpu.VMEM((SSM_NK, SSM_LC * SSM_CB), BF16),
                        pltpu.VMEM((SSM_NK, SSM_LC * SSM_CB), F32),
                        pltpu.VMEM((SSM_NK, ns2), F32),
                        pltpu.VMEM((SSM_NK, ns2), F32),
                        pltpu.VMEM((SSM_NK, ns2), BF16),
                        pltpu.VMEM((SSM_K2, ns2), F32),
                        pltpu.VMEM((SUBLANES, ns2), F32),
                        pltpu.VMEM((SSM_TT, LANES), F32),
                        pltpu.VMEM((SSM_TT, LANES), F32)],
        compiler_params=_cparams(("arbitrary", "arbitrary")),
        name="s5_scan",
    )(proj, proj, kall, pstack, qcat, ltab, d_skip)


def _ssm_tables(a_re, a_im, log_dt, b_re, b_im, c_re, c_im):
    nl, g, p = a_re.shape
    gc = SSM_GROUP
    gb = SSM_GB
    nb = g // gb
    hi = lax.Precision.HIGHEST
    delta = jnp.exp(log_dt)[..., None]
    mag = jnp.exp(delta * a_re)
    ang = delta * a_im
    lam = (mag * jnp.cos(ang), mag * jnp.sin(ang))
    den = a_re * a_re + a_im * a_im
    zr, zi = lam[0] - 1.0, lam[1]
    coef_re = (zr * a_re + zi * a_im) / den
    coef_im = (zi * a_re - zr * a_im) / den
    bt_re = coef_re[..., None] * b_re - coef_im[..., None] * b_im
    bt_im = coef_re[..., None] * b_im + coef_im[..., None] * b_re

    def cmul(x, y):
        return x[0] * y[0] - x[1] * y[1], x[0] * y[1] + x[1] * y[0]

    def power_table(base):
        tab = (jnp.stack([jnp.ones_like(base[0]), base[0]]), jnp.stack([jnp.zeros_like(base[1]), base[1]]))
        step = cmul(base, base)
        while tab[0].shape[0] < SSM_NPOW:
            nxt = cmul(tab, (step[0][None], step[1][None]))
            tab = (jnp.concatenate([tab[0], nxt[0]]), jnp.concatenate([tab[1], nxt[1]]))
            step = cmul(step, step)
        return tab

    pw = power_table(lam)
    pw16 = power_table((pw[0][SSM_LC], pw[1][SSM_LC]))
    eye = jnp.eye(gb, dtype=F32)

    wr = pw[0][:SSM_LC, ..., None] * bt_re[None] - pw[1][:SSM_LC, ..., None] * bt_im[None]
    wi = pw[0][:SSM_LC, ..., None] * bt_im[None] + pw[1][:SSM_LC, ..., None] * bt_re[None]

    k = (jnp.einsum("lgop,tlgpi->tlgio", c_re, wr, precision=hi)
         - jnp.einsum("lgop,tlgpi->tlgio", c_im, wi, precision=hi))
    k = k.reshape(SSM_LC, nl, nb, gb, gc, gc)
    k = k[:, :, :, :, :, None, :] * eye[None, None, None, :, None, :, None]
    kall = k[::-1].transpose(1, 2, 0, 3, 4, 5, 6).reshape(nl, nb, SSM_LC * SSM_CB, SSM_CB).astype(BF16)

    def p_embed(w):
        w = w[SSM_LB - 1::-1].reshape(SSM_LB, nl, nb, gb, p, gc).transpose(1, 2, 0, 3, 5, 4)
        return w[..., :, None, :] * eye[:, None, :, None]

    pstack = jnp.stack([p_embed(wr), p_embed(wi)], axis=5)
    pstack = pstack.reshape(nl, nb, SSM_LB * SSM_CB, 2 * SSM_NS).astype(BF16)

    qw = (pw[0][1:SSM_LB + 1, :, :, None, :], pw[1][1:SSM_LB + 1, :, :, None, :])
    dr = c_re[None] * qw[0] - c_im[None] * qw[1]
    di = c_re[None] * qw[1] + c_im[None] * qw[0]

    def q_embed(w):
        w = w.reshape(SSM_LB, nl, nb, gb, gc, p).transpose(1, 2, 3, 5, 0, 4)
        return w[:, :, :, :, :, None, :] * eye[None, None, :, None, None, :, None]

    qcat = jnp.stack([q_embed(dr), q_embed(-di)], axis=2)
    qcat = qcat.reshape(nl, nb, 2 * SSM_NS, SSM_LB * SSM_CB).astype(BF16)

    def lrows(part):
        r = jnp.concatenate([pw[part][SSM_LB:SSM_LC:SSM_LB], pw16[part][SSM_S2:SSM_S2 + 1],
                             pw16[part][:SSM_S2]])
        r = jnp.concatenate([r, jnp.zeros((LT_ROWS - r.shape[0],) + r.shape[1:], F32)])
        return r.reshape(LT_ROWS, nl, nb, SSM_NS).transpose(1, 2, 0, 3)

    ltab = jnp.concatenate([lrows(0), lrows(1)], axis=3)
    return kall, pstack, qcat, ltab


def _router_kernel(a_ref, w_ref, b_ref, info_ref, infot_ref, k16_ref):
    tm = a_ref.shape[0]
    lg = jnp.dot(a_ref[...], w_ref[...], preferred_element_type=F32) + b_ref[...]
    col = lax.broadcasted_iota(I32, lg.shape, 1)
    big = jnp.int32(LANES)
    neg = -jnp.inf
    is_g = col < N_GROUPS
    lgg = jnp.where(is_g, lg, neg)
    gmax = jnp.max(lgg, axis=1, keepdims=True)
    gsel = jnp.min(jnp.where(is_g & (lgg == gmax), col, big), axis=1, keepdims=True)
    denom = jnp.sum(jnp.where(is_g, jnp.exp(lgg - gmax), 0.0), axis=1, keepdims=True)
    p_sel = 1.0 / denom
    ecol = col - N_GROUPS
    egrp = lax.shift_right_arithmetic(ecol, jnp.int32(int(math.log2(EXPERTS_PER_GROUP))))
    in_grp = (ecol >= 0) & (ecol < N_EXPERTS) & (egrp == gsel)
    le = jnp.where(in_grp, lg, neg)
    m1 = jnp.max(le, axis=1, keepdims=True)
    i1 = jnp.min(jnp.where(in_grp & (le == m1), col, big), axis=1, keepdims=True)
    rest = in_grp & (col != i1)
    le2 = jnp.where(rest, lg, neg)
    m2 = jnp.max(le2, axis=1, keepdims=True)
    i2 = jnp.min(jnp.where(rest & (le2 == m2), col, big), axis=1, keepdims=True)
    e2x = jnp.exp(m2 - m1)
    w1 = p_sel / (1.0 + e2x)
    w2 = p_sel * e2x / (1.0 + e2x)

    oh1 = col == (i1 - N_GROUPS)
    oh2 = col == (i2 - N_GROUPS)
    oh = jnp.where(oh1 | oh2, 1.0, 0.0)
    r_i = lax.broadcasted_iota(I32, (tm, tm), 0)
    c_i = lax.broadcasted_iota(I32, (tm, tm), 1)
    earlier = jnp.where(r_i > c_i, 1.0, 0.0).astype(BF16)
    before = jnp.dot(earlier, oh.astype(BF16), preferred_element_type=F32)
    cnt = jnp.sum(oh, axis=0, keepdims=True)
    k16 = jnp.floor((cnt + (BAND - 1.0)) * (1.0 / BAND))
    k16_8 = jnp.broadcast_to(k16, (SUBLANES, LANES))
    e_r = lax.broadcasted_iota(I32, (LANES, LANES), 0)
    e_c = lax.broadcasted_iota(I32, (LANES, LANES), 1)
    lower_e = jnp.where(e_r < e_c, 1.0, 0.0).astype(BF16)
    seg_start = jnp.dot(k16_8.astype(BF16), lower_e, preferred_element_type=F32)[0:1, :] * BAND
    base = before + seg_start
    lp1 = jnp.sum(jnp.where(oh1, base, 0.0), axis=1, keepdims=True)
    lp2 = jnp.sum(jnp.where(oh2, base, 0.0), axis=1, keepdims=True)

    info = jnp.where(col == 0, lp1, jnp.where(col == 1, lp2, jnp.where(col == 2, w1, jnp.where(col == 3, w2, 0.0))))
    info_ref[...] = info
    infot_ref[...] = info.T[0:SUBLANES, :]
    k16_ref[...] = k16_8


def _router_call(h, w_r, b_r, l):
    t, d = h.shape
    tm = MOE_TM
    nt = t // tm
    return pl.pallas_call(
        _router_kernel,
        grid=(nt,),
        in_specs=[pl.BlockSpec((tm, d), lambda i: (i, 0)),
                  pl.BlockSpec((None, d, LANES), lambda i: (l, 0, 0)),
                  pl.BlockSpec((None, 1, LANES), lambda i: (l, 0, 0))],
        out_specs=[pl.BlockSpec((tm, LANES), lambda i: (i, 0)),
                   pl.BlockSpec((None, SUBLANES, tm), lambda i: (i, 0, 0)),
                   pl.BlockSpec((None, SUBLANES, LANES), lambda i: (i, 0, 0))],
        out_shape=[jax.ShapeDtypeStruct((t, LANES), F32),
                   jax.ShapeDtypeStruct((nt, SUBLANES, tm), F32),
                   jax.ShapeDtypeStruct((nt, SUBLANES, LANES), F32)],
        compiler_params=_cparams(("parallel",)),
        name="moe_router",
    )(h, w_r, b_r)


def _moe_plan(k16, n_chunks):
    k = k16[:, 0, :N_EXPERTS].astype(I32)
    bpc = MOE_TMG // BAND
    tot = jnp.sum(k, axis=0)
    padc = ((tot + bpc - 1) // bpc) * bpc
    end_e = jnp.cumsum(padc)
    start_e = end_e - padc
    gstart = start_e[None, :] + jnp.cumsum(k, axis=0) - k
    lend = jnp.cumsum(k, axis=1)
    lstart = lend - k
    b = jnp.arange(MOE_MAXB, dtype=I32)
    eb = jnp.sum((b[None, :, None] >= lend[:, None, :]).astype(I32), axis=2)
    eb = jnp.minimum(eb, N_EXPERTS - 1)
    dstb = jnp.take_along_axis(gstart, eb, axis=1) + b[None, :] - jnp.take_along_axis(lstart, eb, axis=1)
    nbands = lend[:, -1]
    dstb = jnp.where(b[None, :] < nbands[:, None], dstb, 0)
    cstart = jnp.arange(n_chunks, dtype=I32) * bpc
    cexp = jnp.sum((cstart[:, None] >= end_e[None, :]).astype(I32), axis=1)
    cexp = jnp.minimum(cexp, N_EXPERTS - 1)
    n_used = (end_e[-1] // bpc).reshape(1)
    return dstb.reshape(-1), nbands, cexp, n_used


def _band_copy_out(xloc, xs_ref, dstb_ref, sem, tile, b):
    src = xloc.at[pl.ds(pl.multiple_of(b * BAND, BAND), BAND), :]
    row = pl.multiple_of(dstb_ref[tile * MOE_MAXB + b] * BAND, BAND)
    return pltpu.make_async_copy(src, xs_ref.at[pl.ds(row, BAND), :], sem)


def _dispatch_kernel(dstb_ref, nb_ref, h_ref, lpt_ref, xs_in_ref, xs_ref, xloc, sem):
    del xs_in_ref
    tile = pl.program_id(0)
    tm, d = h_ref.shape
    lp1 = lpt_ref[0:1, :]
    lp2 = lpt_ref[1:2, :]
    rows = lax.broadcasted_iota(I32, (MOE_MAXR, tm), 0).astype(F32)
    perm = jnp.where((rows == lp1) | (rows == lp2), 1.0, 0.0).astype(BF16)
    for c0 in range(0, d, MOE_COLS):
        xloc[:, c0:c0 + MOE_COLS] = jnp.dot(
            perm, h_ref[:, c0:c0 + MOE_COLS], preferred_element_type=F32).astype(BF16)
    n = nb_ref[tile]

    def start(b, carry):
        _band_copy_out(xloc, xs_ref, dstb_ref, sem, tile, b).start()
        return carry

    def wait(b, carry):
        _band_copy_out(xloc, xs_ref, dstb_ref, sem, tile, b).wait()
        return carry

    lax.fori_loop(0, n, start, 0)
    lax.fori_loop(0, n, wait, 0)


def _dispatch_call(h, infot, dstb, nbands, n_rows):
    t, d = h.shape
    nt = t // MOE_TM
    xs0 = jnp.zeros((n_rows, d), BF16)
    return pl.pallas_call(
        _dispatch_kernel,
        grid_spec=pltpu.PrefetchScalarGridSpec(
            num_scalar_prefetch=2,
            grid=(nt,),
            in_specs=[pl.BlockSpec((MOE_TM, d), lambda i, *_: (i, 0)),
                      pl.BlockSpec((None, SUBLANES, MOE_TM), lambda i, *_: (i, 0, 0)),
                      pl.BlockSpec(memory_space=pl.ANY)],
            out_specs=pl.BlockSpec(memory_space=pl.ANY),
            scratch_shapes=[pltpu.VMEM((MOE_MAXR, d), BF16), pltpu.SemaphoreType.DMA(())]),
        out_shape=jax.ShapeDtypeStruct((n_rows, d), BF16),
        input_output_aliases={4: 0},
        compiler_params=_cparams(("arbitrary",)),
        name="moe_dispatch",
    )(dstb, nbands, h, infot, xs0)


def _experts_kernel(cexp_ref, nused_ref, x_ref, wgu_ref, wd_ref, y_ref):
    del cexp_ref
    f = wd_ref.shape[0]
    c = pl.program_id(0)

    @pl.when(c < nused_ref[0])
    def _():
        gu = jnp.dot(x_ref[...], wgu_ref[...], preferred_element_type=F32)
        g, u = gu[:, 0:f], gu[:, f:]
        act = ((g * jax.nn.sigmoid(g)) * u).astype(BF16)
        y_ref[...] = jnp.dot(act, wd_ref[...], preferred_element_type=F32).astype(y_ref.dtype)

    @pl.when(c >= nused_ref[0])
    def _():
        y_ref[...] = jnp.zeros_like(y_ref)


def _experts_call(xs, wgu, wd, l, cexp, n_used):
    n_rows, d = xs.shape
    f2 = wgu.shape[3]
    f = wd.shape[2]
    n_chunks = n_rows // MOE_TMG
    return pl.pallas_call(
        _experts_kernel,
        grid_spec=pltpu.PrefetchScalarGridSpec(
            num_scalar_prefetch=2,
            grid=(n_chunks,),
            in_specs=[pl.BlockSpec((MOE_TMG, d), lambda c, ce, nu: (jnp.minimum(c, nu[0] - 1), 0)),
                      pl.BlockSpec((None, None, d, f2), lambda c, ce, nu: (l, ce[c], 0, 0)),
                      pl.BlockSpec((None, None, f, d), lambda c, ce, nu: (l, ce[c], 0, 0))],
            out_specs=pl.BlockSpec((MOE_TMG, d), lambda c, ce, nu: (c, 0))),
        out_shape=jax.ShapeDtypeStruct((n_rows, d), BF16),
        compiler_params=_cparams(("arbitrary",)),
        name="moe_experts",
    )(cexp, n_used, xs, wgu, wd)


def _band_copy_in(y_ref, yloc, dstb_ref, sem, tile, b):
    row = pl.multiple_of(dstb_ref[tile * MOE_MAXB + b] * BAND, BAND)
    dst = yloc.at[pl.ds(pl.multiple_of(b * BAND, BAND), BAND), :]
    return pltpu.make_async_copy(y_ref.at[pl.ds(row, BAND), :], dst, sem)


def _combine_kernel(dstb_ref, nb_ref, y_ref, info_ref, x_ref, g_ref, o_ref, yloc, sem):
    tile = pl.program_id(0)
    tm, d = x_ref.shape
    n = nb_ref[tile]

    def start(b, carry):
        _band_copy_in(y_ref, yloc, dstb_ref, sem, tile, b).start()
        return carry

    def wait(b, carry):
        _band_copy_in(y_ref, yloc, dstb_ref, sem, tile, b).wait()
        return carry

    def clear(b, carry):
        yloc[pl.ds(pl.multiple_of(b * BAND, BAND), BAND), :] = jnp.zeros((BAND, d), yloc.dtype)
        return carry

    lax.fori_loop(0, n, start, 0)
    lax.fori_loop(n, MOE_MAXB, clear, 0)
    info = info_ref[...]
    lp1, lp2, w1, w2 = info[:, 0:1], info[:, 1:2], info[:, 2:3], info[:, 3:4]
    cols = lax.broadcasted_iota(I32, (tm, MOE_MAXR), 1).astype(F32)
    pw = (jnp.where(cols == lp1, w1, 0.0) + jnp.where(cols == lp2, w2, 0.0)).astype(BF16)
    lax.fori_loop(0, n, wait, 0)
    for c0 in range(0, d, MOE_COLS):
        sl = slice(c0, c0 + MOE_COLS)
        acc = jnp.dot(pw, yloc[:, sl], preferred_element_type=F32)
        o_ref[:, sl] = x_ref[:, sl] + g_ref[:, sl] * acc


def _combine_call(y, info, x, mod, l, gate_blk, dstb, nbands):
    t, d = x.shape
    nt = t // MOE_TM
    return pl.pallas_call(
        _combine_kernel,
        grid_spec=pltpu.PrefetchScalarGridSpec(
            num_scalar_prefetch=2,
            grid=(nt,),
            in_specs=[pl.BlockSpec(memory_space=pl.ANY),
                      pl.BlockSpec((MOE_TM, LANES), lambda i, *_: (i, 0)),
                      pl.BlockSpec((MOE_TM, d), lambda i, *_: (i, 0)),
                      pl.BlockSpec((None, 1, d), lambda i, *_: (l, 0, gate_blk))],
            out_specs=pl.BlockSpec((MOE_TM, d), lambda i, *_: (i, 0)),
            scratch_shapes=[pltpu.VMEM((MOE_MAXR, d), BF16), pltpu.SemaphoreType.DMA(())]),
        out_shape=jax.ShapeDtypeStruct((t, d), F32),
        compiler_params=_cparams(("arbitrary",)),
        name="moe_combine",
    )(dstb, nbands, y, info, x, mod)


def kernel(x, c, w_mod, mod_table, norm1_g, w_in, ssm_a_re, ssm_a_im, ssm_log_dt, ssm_b_re, ssm_b_im, ssm_c_re, ssm_c_im, ssm_d, w_glu, w_br_ssm, conv_w, w_br_conv, w_o, norm2_g, w_router_group, b_router_group, w_router_expert, b_router_expert, w_exp_gate, w_exp_up, w_exp_down, final_g):
    bsz, seq, d = x.shape
    depth = mod_table.shape[0]
    h_ssm = ssm_d.shape[1]
    h_conv = conv_w.shape[2]
    n_exp, _, d_expert = w_exp_gate.shape[1:]
    assert bsz == 1 and seq % SSM_TT == 0 and h_ssm % SSM_CB == 0 and seq % MOE_TM == 0
    assert ssm_a_re.shape[1:] == (h_ssm // SSM_GROUP, SSM_STATE) and n_exp == N_EXPERTS
    assert d % MOE_COLS == 0
    u_col, b_col, c_col, v_col = 0, h_ssm, h_ssm + h_conv, h_ssm + 2 * h_conv
    gs_col = h_ssm + 3 * h_conv
    gc_col = gs_col + d
    sh1, sc1, g1, sh2, sc2, g2 = range(N_MOD)

    w_in_b = w_in.astype(BF16)
    w_glu_b = w_glu.astype(BF16)
    w_brs_b = w_br_ssm.astype(BF16)
    w_brc_b = w_br_conv.astype(BF16)
    w_o_b = w_o.astype(BF16)
    wgu_b = jnp.concatenate([w_exp_gate, w_exp_up], axis=3).astype(BF16)
    wd_b = w_exp_down.astype(BF16)
    pad = LANES - N_GROUPS - N_EXPERTS
    w_r = jnp.concatenate([w_router_group, w_router_expert, jnp.zeros((depth, d, pad), F32)], axis=2).astype(BF16)
    b_r = jnp.concatenate([b_router_group, b_router_expert, jnp.zeros((depth, pad), F32)], axis=1)[:, None, :]
    kall, pstack, qcat, ltab = _ssm_tables(ssm_a_re, ssm_a_im, ssm_log_dt, ssm_b_re, ssm_b_im,
                                           ssm_c_re, ssm_c_im)

    n_tiles = seq // MOE_TM
    n_rows = n_tiles * MOE_MAXR + N_EXPERTS * MOE_TMG
    n_chunks = n_rows // MOE_TMG

    xt = x.reshape(seq, d)
    mod = _mod_call(c, w_mod, mod_table)[:, None, :]
    norm1_g, norm2_g, ssm_d = norm1_g[:, None, :], norm2_g[:, None, :], ssm_d[:, None, :]

    for l in range(depth):
        h1 = _norm_mod_call(xt, norm1_g, mod, l, sc1, sh1)
        proj = _mm_call(h1, w_in_b, l, F32, 1024, 512, "in_proj")
        y_pre = _ssm_call(proj, u_col, h_ssm, kall, pstack, qcat, ltab, ssm_d, l)
        ys = _glu_call(y_pre, w_glu_b, l)
        yc = _conv_call(proj, conv_w, l, b_col, c_col, v_col, h_conv)
        merged = _merge_call(ys, yc, w_brs_b, w_brc_b, l, proj, gs_col, gc_col)
        xt = _resid_call(merged, w_o_b, l, xt, mod, g1, 1024, 512, "out_proj")

        h2 = _norm_mod_call(xt, norm2_g, mod, l, sc2, sh2)
        info, infot, k16 = _router_call(h2, w_r, b_r, l)
        dstb, nbands, cexp, n_used = _moe_plan(k16, n_chunks)
        xs = _dispatch_call(h2, infot, dstb, nbands, n_rows)
        ye = _experts_call(xs, wgu_b, wd_b, l, cexp, n_used)
        xt = _combine_call(ye, info, xt, mod, l, g2, dstb, nbands)

    out = _norm_call(xt, final_g[None, :])
    return out.reshape(bsz, seq, d)
```

```python
import functools
import math

import jax
import jax.numpy as jnp
from jax import lax
from jax.experimental import pallas as pl
from jax.experimental.pallas import tpu as pltpu

F32 = jnp.float32
BF16 = jnp.bfloat16
I32 = jnp.int32

EPS = 1e-6
SSM_GROUP = 16
SSM_STATE = 64
CONV_WIDTH = 3
N_GROUPS = 4
EXPERTS_PER_GROUP = 4
N_EXPERTS = N_GROUPS * EXPERTS_PER_GROUP
N_MOD = 6

LANES = 128
SUBLANES = 8
VMEM_LIMIT_BYTES = 58 * 1024 * 1024

SSM_LC = 16
SSM_LA = 4
SSM_LB = 4
SSM_S2 = 16
SSM_K2 = 16
SSM_NK = SSM_S2 * SSM_K2
SSM_TT = SSM_NK * SSM_LC
SSM_CB = 256
SSM_GB = SSM_CB // SSM_GROUP
SSM_NS = SSM_GB * SSM_STATE
LT_ROWS = 24
SSM_NPOW = 32

MOE_TM = 512
BAND = 2 * SUBLANES
MOE_MAXR = 2 * MOE_TM + N_EXPERTS * BAND
MOE_MAXB = MOE_MAXR // BAND
MOE_TMG = 512
MOE_COLS = 1024


def _cparams(sem):
    return pltpu.CompilerParams(dimension_semantics=sem, vmem_limit_bytes=VMEM_LIMIT_BYTES)


def _mod_kernel(c_ref, w_ref, t_ref, o_ref):
    c = c_ref[...]
    s = c * jax.nn.sigmoid(c)
    shared = jnp.sum(s * w_ref[...], axis=0, keepdims=True)
    o_ref[...] = t_ref[...] + shared


def _mod_call(c, w_mod, mod_table):
    d, n = w_mod.shape
    depth = mod_table.shape[0]
    tn = 512
    return pl.pallas_call(
        _mod_kernel,
        grid=(n // tn,),
        in_specs=[pl.BlockSpec((d, 1), lambda j: (0, 0)),
                  pl.BlockSpec((d, tn), lambda j: (0, j)),
                  pl.BlockSpec((depth, tn), lambda j: (0, j))],
        out_specs=pl.BlockSpec((depth, tn), lambda j: (0, j)),
        out_shape=jax.ShapeDtypeStruct((depth, n), F32),
        compiler_params=_cparams(("arbitrary",)),
        name="adaln_mod",
    )(c.reshape(d, 1), w_mod, mod_table)


def _norm_mod_kernel(x_ref, g_ref, sc_ref, sh_ref, o_ref):
    x = x_ref[...]
    y = x * lax.rsqrt(jnp.mean(x * x, axis=-1, keepdims=True) + EPS)
    o_ref[...] = ((y * g_ref[...]) * (1.0 + sc_ref[...]) + sh_ref[...]).astype(o_ref.dtype)


def _norm_kernel(x_ref, g_ref, o_ref):
    x = x_ref[...]
    y = x * lax.rsqrt(jnp.mean(x * x, axis=-1, keepdims=True) + EPS)
    o_ref[...] = (y * g_ref[...]).astype(o_ref.dtype)


def _norm_mod_call(x, gains, mod, l, sc_blk, sh_blk, tm=256):
    t, d = x.shape
    return pl.pallas_call(
        _norm_mod_kernel,
        grid=(t // tm,),
        in_specs=[pl.BlockSpec((tm, d), lambda i: (i, 0)),
                  pl.BlockSpec((None, 1, d), lambda i: (l, 0, 0)),
                  pl.BlockSpec((None, 1, d), lambda i: (l, 0, sc_blk)),
                  pl.BlockSpec((None, 1, d), lambda i: (l, 0, sh_blk))],
        out_specs=pl.BlockSpec((tm, d), lambda i: (i, 0)),
        out_shape=jax.ShapeDtypeStruct((t, d), BF16),
        compiler_params=_cparams(("parallel",)),
        name="norm_mod",
    )(x, gains, mod, mod)


def _norm_call(x, g, tm=256):
    t, d = x.shape
    return pl.pallas_call(
        _norm_kernel,
        grid=(t // tm,),
        in_specs=[pl.BlockSpec((tm, d), lambda i: (i, 0)), pl.BlockSpec((1, d), lambda i: (0, 0))],
        out_specs=pl.BlockSpec((tm, d), lambda i: (i, 0)),
        out_shape=jax.ShapeDtypeStruct((t, d), F32),
        compiler_params=_cparams(("parallel",)),
        name="final_norm",
    )(x, g)


def _wspec(l, k, tn):
    return pl.BlockSpec((None, k, tn), lambda i, j: (l, 0, j))


def _mm_kernel(a_ref, w_ref, o_ref):
    o_ref[...] = jnp.dot(a_ref[...], w_ref[...], preferred_element_type=F32).astype(o_ref.dtype)


def _mm_call(a, w, l, out_dtype, tm, tn, name):
    t, k = a.shape
    n = w.shape[2]
    return pl.pallas_call(
        _mm_kernel,
        grid=(t // tm, n // tn),
        in_specs=[pl.BlockSpec((tm, k), lambda i, j: (i, 0)), _wspec(l, k, tn)],
        out_specs=pl.BlockSpec((tm, tn), lambda i, j: (i, j)),
        out_shape=jax.ShapeDtypeStruct((t, n), out_dtype),
        compiler_params=_cparams(("parallel", "arbitrary")),
        name=name,
    )(a, w)


def _glu_kernel(a_ref, y_ref, w_ref, o_ref):
    acc = jnp.dot(a_ref[...], w_ref[...], preferred_element_type=F32)
    o_ref[...] = (y_ref[...].astype(F32) * jax.nn.sigmoid(acc)).astype(o_ref.dtype)


def _glu_call(y, w, l, tm=1024, tn=512):
    t, k = y.shape
    n = w.shape[2]
    return pl.pallas_call(
        _glu_kernel,
        grid=(t // tm, n // tn),
        in_specs=[pl.BlockSpec((tm, k), lambda i, j: (i, 0)),
                  pl.BlockSpec((tm, tn), lambda i, j: (i, j)),
                  _wspec(l, k, tn)],
        out_specs=pl.BlockSpec((tm, tn), lambda i, j: (i, j)),
        out_shape=jax.ShapeDtypeStruct((t, n), BF16),
        compiler_params=_cparams(("parallel", "arbitrary")),
        name="s5_glu",
    )(y, y, w)


def _merge_kernel(ys_ref, yc_ref, ws_ref, wc_ref, gs_ref, gc_ref, o_ref):
    y_ssm = jnp.dot(ys_ref[...], ws_ref[...], preferred_element_type=F32)
    y_conv = jnp.dot(yc_ref[...], wc_ref[...], preferred_element_type=F32)
    merged = jax.nn.sigmoid(gs_ref[...]) * y_ssm + jax.nn.sigmoid(gc_ref[...]) * y_conv
    o_ref[...] = merged.astype(o_ref.dtype)


def _merge_call(ys, yc, ws, wc, l, proj, gs_col, gc_col, tm=1024, tn=512):
    t, k = ys.shape
    n = ws.shape[2]
    gs_blk, gc_blk = gs_col // tn, gc_col // tn
    return pl.pallas_call(
        _merge_kernel,
        grid=(t // tm, n // tn),
        in_specs=[pl.BlockSpec((tm, k), lambda i, j: (i, 0)),
                  pl.BlockSpec((tm, k), lambda i, j: (i, 0)),
                  _wspec(l, k, tn),
                  _wspec(l, k, tn),
                  pl.BlockSpec((tm, tn), lambda i, j: (i, gs_blk + j)),
                  pl.BlockSpec((tm, tn), lambda i, j: (i, gc_blk + j))],
        out_specs=pl.BlockSpec((tm, tn), lambda i, j: (i, j)),
        out_shape=jax.ShapeDtypeStruct((t, n), BF16),
        compiler_params=_cparams(("parallel", "arbitrary")),
        name="branch_merge",
    )(ys, yc, ws, wc, proj, proj)


def _resid_kernel(a_ref, w_ref, x_ref, g_ref, o_ref):
    acc = jnp.dot(a_ref[...], w_ref[...], preferred_element_type=F32)
    o_ref[...] = x_ref[...] + g_ref[...] * acc


def _resid_call(a, w, l, x, mod, gate_blk, tm, tn, name):
    t, k = a.shape
    n = w.shape[2]
    gb = gate_blk * (n // tn)
    return pl.pallas_call(
        _resid_kernel,
        grid=(t // tm, n // tn),
        in_specs=[pl.BlockSpec((tm, k), lambda i, j: (i, 0)),
                  _wspec(l, k, tn),
                  pl.BlockSpec((tm, tn), lambda i, j: (i, j)),
                  pl.BlockSpec((None, 1, tn), lambda i, j: (l, 0, gb + j))],
        out_specs=pl.BlockSpec((tm, tn), lambda i, j: (i, j)),
        out_shape=jax.ShapeDtypeStruct((t, n), F32),
        compiler_params=_cparams(("parallel", "arbitrary")),
        name=name,
    )(a, w, x, mod)


def _conv_kernel(b_ref, c_ref, v_ref, ch_ref, vh_ref, w_ref, o_ref, z_scr):
    tm = b_ref.shape[0]
    halo = SUBLANES
    z = c_ref[...] * v_ref[...]
    zh = jnp.where(pl.program_id(0) == 0, 0.0, ch_ref[...] * vh_ref[...])
    z_scr[0:halo, :] = zh
    z_scr[halo:, :] = z
    w = w_ref[...]
    conv = w[2:3, :] * z
    for k in range(CONV_WIDTH - 1):
        conv = conv + w[k:k + 1, :] * z_scr[pl.ds(halo - (CONV_WIDTH - 1 - k), tm), :]
    o_ref[...] = (b_ref[...] * conv).astype(o_ref.dtype)


def _conv_call(proj, conv_w, l, b_col, c_col, v_col, width, tm=512, tc=512):
    t = proj.shape[0]
    bb, cb, vb = b_col // tc, c_col // tc, v_col // tc
    hb = tm // SUBLANES

    def halo_map(off):
        return lambda i, j: (jnp.maximum(i * hb - 1, 0), off + j)

    return pl.pallas_call(
        _conv_kernel,
        grid=(t // tm, width // tc),
        in_specs=[pl.BlockSpec((tm, tc), lambda i, j: (i, bb + j)),
                  pl.BlockSpec((tm, tc), lambda i, j: (i, cb + j)),
                  pl.BlockSpec((tm, tc), lambda i, j: (i, vb + j)),
                  pl.BlockSpec((SUBLANES, tc), halo_map(cb)),
                  pl.BlockSpec((SUBLANES, tc), halo_map(vb)),
                  pl.BlockSpec((None, CONV_WIDTH, tc), lambda i, j: (l, 0, j))],
        out_specs=pl.BlockSpec((tm, tc), lambda i, j: (i, j)),
        out_shape=jax.ShapeDtypeStruct((t, width), BF16),
        scratch_shapes=[pltpu.VMEM((tm + SUBLANES, tc), F32)],
        compiler_params=_cparams(("parallel", "parallel")),
        name="short_conv",
    )(proj, proj, proj, proj, proj, conv_w)


def _cmul_row(xr, xi, lr, li):
    return xr * lr - xi * li, xr * li + xi * lr


def _gelu_tanh(x):
    return 0.5 * x * (1.0 + jnp.tanh(math.sqrt(2.0 / math.pi) * (x + 0.044715 * (x * x * x))))


def _expand_block_diag(dst, spread_rows, row_shift, col_shift, chunk):
    n_r, n_c = dst.shape
    gmask = SSM_GB - 1
    for r0 in range(0, n_r, chunk):
        spread = spread_rows(r0, chunk)
        rg = (lax.broadcasted_iota(I32, (chunk, n_c), 0) + r0) >> row_shift
        cg = lax.broadcasted_iota(I32, (chunk, n_c), 1) >> col_shift
        keep = (rg & gmask) == (cg & gmask)
        dst[r0:r0 + chunk, :] = jnp.where(keep, spread, 0.0).astype(dst.dtype)


def _ssm_build_tables(pc_ref, qc_ref, kc_ref, p_scr, q_scr, k_scr):
    ns2 = 2 * SSM_NS
    p_state = SSM_STATE
    lp, lc = int(math.log2(p_state)), int(math.log2(SSM_GROUP))
    lns = int(math.log2(SSM_NS))
    r = lax.broadcasted_iota(I32, (LANES, ns2), 0)
    c = lax.broadcasted_iota(I32, (LANES, ns2), 1)
    tile_p = jnp.where(((r >> lp) == (c >> lns)) & ((r & (p_state - 1)) == (c & (p_state - 1))),
                       1.0, 0.0).astype(BF16)

    def p_rows(r0, chunk):
        return jnp.dot(pc_ref[r0:r0 + chunk, :], tile_p, preferred_element_type=F32)

    _expand_block_diag(p_scr, p_rows, lc, lp, 256)
    r = lax.broadcasted_iota(I32, (ns2, LANES), 0)
    c = lax.broadcasted_iota(I32, (ns2, LANES), 1)
    tile_q = jnp.where(((c >> lp) == (r >> lns)) & ((c & (p_state - 1)) == (r & (p_state - 1))),
                       1.0, 0.0).astype(BF16)

    def q_rows(r0, chunk):
        return jnp.dot(tile_q[r0:r0 + chunk, :], qc_ref[...], preferred_element_type=F32)

    _expand_block_diag(q_scr, q_rows, lp, lc, 512)
    r = lax.broadcasted_iota(I32, (LANES, SSM_CB), 0)
    c = lax.broadcasted_iota(I32, (LANES, SSM_CB), 1)
    tile_k = jnp.where(r == (c & (SSM_GROUP - 1)), 1.0, 0.0).astype(BF16)

    def k_rows(r0, chunk):
        return jnp.dot(kc_ref[r0:r0 + chunk, :], tile_k, preferred_element_type=F32)

    _expand_block_diag(k_scr, k_rows, lc, lc, 1024)


def _ssm_kernel(u0_ref, u1_ref, kc_ref, pc_ref, qc_ref, lt_ref, d_ref, o_ref,
                ub2, y2, bx, s_scr, ha, cs, carry, yo0, yo1, kall_ref, p_ref, q_ref):
    ns = SSM_NS
    cb = SSM_CB
    rows = SSM_K2
    kstride = SSM_S2 * SSM_LC
    u_halves = (u0_ref, u1_ref)
    yo = (yo0, yo1)

    @pl.when(pl.program_id(1) == 0)
    def _():
        carry[...] = jnp.zeros_like(carry)
        _ssm_build_tables(pc_ref, qc_ref, kc_ref, p_ref, q_ref, kall_ref)

    def lt(row):
        return lt_ref[row:row + 1, 0:ns], lt_ref[row:row + 1, ns:2 * ns]

    for s in range(SSM_LC):
        for s2 in range(SSM_S2):
            idx = pl.ds(s2 * SSM_LC + s, rows, stride=kstride)
            for hf, u_ref in enumerate(u_halves):
                c0 = s * cb + hf * LANES
                ub2[s2 * rows:(s2 + 1) * rows, c0:c0 + LANES] = u_ref[idx, :].astype(BF16)

    for sp in range(SSM_LC):
        y2[:, sp * cb:(sp + 1) * cb] = jnp.dot(
            ub2[:, 0:(sp + 1) * cb], kall_ref[(SSM_LC - 1 - sp) * cb:, :],
            preferred_element_type=F32)

    s_scr[...] = jnp.dot(ub2[:, (SSM_LA - 1) * SSM_LB * cb:], p_ref[...], preferred_element_type=F32)
    for a in range(SSM_LA - 1):
        bx[...] = jnp.dot(ub2[:, a * SSM_LB * cb:(a + 1) * SSM_LB * cb], p_ref[...],
                          preferred_element_type=F32)
        lr, li = lt(SSM_LA - 2 - a)
        for s2 in range(SSM_S2):
            sl = slice(s2 * rows, (s2 + 1) * rows)
            pr, pi = _cmul_row(bx[sl, 0:ns], bx[sl, ns:], lr, li)
            s_scr[sl, 0:ns] += pr
            s_scr[sl, ns:] += pi

    lr, li = lt(4 + 1)
    for s2 in range(1, SSM_S2):
        pv = slice((s2 - 1) * rows, s2 * rows)
        sl = slice(s2 * rows, (s2 + 1) * rows)
        pr, pi = _cmul_row(s_scr[pv, 0:ns], s_scr[pv, ns:], lr, li)
        s_scr[sl, 0:ns] += pr
        s_scr[sl, ns:] += pi

    lr, li = lt(3)
    last = slice((SSM_S2 - 1) * rows, SSM_S2 * rows)
    zr, zi = s_scr[last, 0:ns], s_scr[last, ns:]
    cr, ci = carry[0:1, 0:ns], carry[0:1, ns:]
    for k2 in range(SSM_K2):
        cs[k2:k2 + 1, 0:ns] = cr
        cs[k2:k2 + 1, ns:] = ci
        pr, pi = _cmul_row(cr, ci, lr, li)
        cr, ci = pr + zr[k2:k2 + 1, :], pi + zi[k2:k2 + 1, :]
    carry[0:1, 0:ns] = cr
    carry[0:1, ns:] = ci

    for a in range(SSM_LA):
        for s2 in range(SSM_S2):
            sl = slice(s2 * rows, (s2 + 1) * rows)
            hr, hi = cs[:, 0:ns], cs[:, ns:]
            if s2 > 0:
                pv = slice((s2 - 1) * rows, s2 * rows)
                lr, li = lt(4 + s2)
                pr, pi = _cmul_row(hr, hi, lr, li)
                hr, hi = s_scr[pv, 0:ns] + pr, s_scr[pv, ns:] + pi
            if a > 0:
                lr, li = lt(a - 1)
                hr, hi = _cmul_row(hr, hi, lr, li)
            ha[sl, 0:ns] = hr.astype(BF16)
            ha[sl, ns:] = hi.astype(BF16)
        wa = SSM_LB * cb
        y2[:, a * wa:(a + 1) * wa] += jnp.dot(ha[...], q_ref[...], preferred_element_type=F32)

    d = d_ref[...]
    for s in range(SSM_LC):
        for s2 in range(SSM_S2):
            idx = pl.ds(s2 * SSM_LC + s, rows, stride=kstride)
            for hf, u_ref in enumerate(u_halves):
                c0 = s * cb + hf * LANES
                y = (y2[s2 * rows:(s2 + 1) * rows, c0:c0 + LANES]
                     + d[:, hf * LANES:(hf + 1) * LANES] * u_ref[idx, :])
                yo[hf][idx, :] = _gelu_tanh(y)
    for hf in range(len(u_halves)):
        o_ref[:, hf * LANES:(hf + 1) * LANES] = yo[hf][...].astype(o_ref.dtype)


def _ssm_call(proj, u_col, h_ssm, kc, pc, qc, ltab, d_skip, l):
    t = proj.shape[0]
    nb = h_ssm // SSM_CB
    ub = u_col // SSM_CB
    ns2 = 2 * SSM_NS

    def tab(r, c):
        return pl.BlockSpec((None, None, r, c), lambda b, i: (l, b, 0, 0))

    return pl.pallas_call(
        _ssm_kernel,
        grid=(nb, t // SSM_TT),
        in_specs=[pl.BlockSpec((SSM_TT, LANES), lambda b, i: (i, 2 * (ub + b))),
                  pl.BlockSpec((SSM_TT, LANES), lambda b, i: (i, 2 * (ub + b) + 1)),
                  tab(SSM_LC * SSM_CB, LANES),
                  tab(SSM_LB * SSM_CB, LANES),
                  tab(LANES, SSM_LB * SSM_CB),
                  tab(LT_ROWS, ns2),
                  pl.BlockSpec((None, 1, SSM_CB), lambda b, i: (l, 0, b))],
        out_specs=pl.BlockSpec((SSM_TT, SSM_CB), lambda b, i: (i, b)),
        out_shape=jax.ShapeDtypeStruct((t, h_ssm), BF16),
        scratch_shapes=[pltpu.VMEM((SSM_NK, SSM_LC * SSM_CB), BF16),
                        pltpu.VMEM((SSM_NK, SSM_LC * SSM_CB), F32),
                        pltpu.VMEM((SSM_NK, ns2), F32),
                        pltpu.VMEM((SSM_NK, ns2), F32),
                        pltpu.VMEM((SSM_NK, ns2), BF16),
                        pltpu.VMEM((SSM_K2, ns2), F32),
                        pltpu.VMEM((SUBLANES, ns2), F32),
                        pltpu.VMEM((SSM_TT, LANES), F32),
                        pltpu.VMEM((SSM_TT, LANES), F32),
                        pltpu.VMEM((SSM_LC * SSM_CB, SSM_CB), BF16),
                        pltpu.VMEM((SSM_LB * SSM_CB, ns2), BF16),
                        pltpu.VMEM((ns2, SSM_LB * SSM_CB), BF16)],
        compiler_params=_cparams(("arbitrary", "arbitrary")),
        name="s5_scan",
    )(proj, proj, kc, pc, qc, ltab, d_skip)


def _ssm_tables(a_re, a_im, log_dt, b_re, b_im, c_re, c_im):
    assert 2 * a_re.shape[2] == LANES
    nl, g, p = a_re.shape
    gc = SSM_GROUP
    gb = SSM_GB
    nb = g // gb
    hi = lax.Precision.HIGHEST
    delta = jnp.exp(log_dt)[..., None]
    mag = jnp.exp(delta * a_re)
    ang = delta * a_im
    lam = (mag * jnp.cos(ang), mag * jnp.sin(ang))
    den = a_re * a_re + a_im * a_im
    zr, zi = lam[0] - 1.0, lam[1]
    coef_re = (zr * a_re + zi * a_im) / den
    coef_im = (zi * a_re - zr * a_im) / den
    bt_re = coef_re[..., None] * b_re - coef_im[..., None] * b_im
    bt_im = coef_re[..., None] * b_im + coef_im[..., None] * b_re

    def cmul(x, y):
        return x[0] * y[0] - x[1] * y[1], x[0] * y[1] + x[1] * y[0]

    def power_table(base):
        tab = (jnp.stack([jnp.ones_like(base[0]), base[0]]), jnp.stack([jnp.zeros_like(base[1]), base[1]]))
        step = cmul(base, base)
        while tab[0].shape[0] < SSM_NPOW:
            nxt = cmul(tab, (step[0][None], step[1][None]))
            tab = (jnp.concatenate([tab[0], nxt[0]]), jnp.concatenate([tab[1], nxt[1]]))
            step = cmul(step, step)
        return tab

    pw = power_table(lam)
    pw16 = power_table((pw[0][SSM_LC], pw[1][SSM_LC]))

    wr = pw[0][:SSM_LC, ..., None] * bt_re[None] - pw[1][:SSM_LC, ..., None] * bt_im[None]
    wi = pw[0][:SSM_LC, ..., None] * bt_im[None] + pw[1][:SSM_LC, ..., None] * bt_re[None]

    k = (jnp.einsum("lgop,tlgpi->tlgio", c_re, wr, precision=hi)
         - jnp.einsum("lgop,tlgpi->tlgio", c_im, wi, precision=hi))
    k = k[::-1].reshape(SSM_LC, nl, nb, gb, gc, gc).transpose(1, 2, 0, 3, 4, 5)
    kc = k.reshape(nl, nb, SSM_LC * SSM_CB, gc)
    kc = jnp.pad(kc, ((0, 0), (0, 0), (0, 0), (0, LANES - gc))).astype(BF16)

    def p_rows(w):
        return w[SSM_LB - 1::-1].reshape(SSM_LB, nl, nb, gb, p, gc).transpose(1, 2, 0, 3, 5, 4)

    pc = jnp.stack([p_rows(wr), p_rows(wi)], axis=5)
    pc = pc.reshape(nl, nb, SSM_LB * SSM_CB, 2 * p).astype(BF16)

    qw = (pw[0][1:SSM_LB + 1, :, :, None, :], pw[1][1:SSM_LB + 1, :, :, None, :])
    dr = c_re[None] * qw[0] - c_im[None] * qw[1]
    di = c_re[None] * qw[1] + c_im[None] * qw[0]

    def q_cols(w):
        return w.reshape(SSM_LB, nl, nb, gb, gc, p).transpose(1, 2, 5, 0, 3, 4)

    qc = jnp.stack([q_cols(dr), q_cols(-di)], axis=2)
    qc = qc.reshape(nl, nb, 2 * p, SSM_LB * SSM_CB).astype(BF16)

    def lrows(part):
        r = jnp.concatenate([pw[part][SSM_LB:SSM_LC:SSM_LB], pw16[part][SSM_S2:SSM_S2 + 1],
                             pw16[part][:SSM_S2]])
        r = jnp.concatenate([r, jnp.zeros((LT_ROWS - r.shape[0],) + r.shape[1:], F32)])
        return r.reshape(LT_ROWS, nl, nb, SSM_NS).transpose(1, 2, 0, 3)

    ltab = jnp.concatenate([lrows(0), lrows(1)], axis=3)
    return kc, pc, qc, ltab


def _router_kernel(a_ref, w_ref, b_ref, info_ref, infot_ref, k16_ref):
    tm = a_ref.shape[0]
    lg = jnp.dot(a_ref[...], w_ref[...], preferred_element_type=F32) + b_ref[...]
    col = lax.broadcasted_iota(I32, lg.shape, 1)
    big = jnp.int32(LANES)
    neg = -jnp.inf
    is_g = col < N_GROUPS
    lgg = jnp.where(is_g, lg, neg)
    gmax = jnp.max(lgg, axis=1, keepdims=True)
    gsel = jnp.min(jnp.where(is_g & (lgg == gmax), col, big), axis=1, keepdims=True)
    denom = jnp.sum(jnp.where(is_g, jnp.exp(lgg - gmax), 0.0), axis=1, keepdims=True)
    p_sel = 1.0 / denom
    ecol = col - N_GROUPS
    egrp = lax.shift_right_arithmetic(ecol, jnp.int32(int(math.log2(EXPERTS_PER_GROUP))))
    in_grp = (ecol >= 0) & (ecol < N_EXPERTS) & (egrp == gsel)
    le = jnp.where(in_grp, lg, neg)
    m1 = jnp.max(le, axis=1, keepdims=True)
    i1 = jnp.min(jnp.where(in_grp & (le == m1), col, big), axis=1, keepdims=True)
    rest = in_grp & (col != i1)
    le2 = jnp.where(rest, lg, neg)
    m2 = jnp.max(le2, axis=1, keepdims=True)
    i2 = jnp.min(jnp.where(rest & (le2 == m2), col, big), axis=1, keepdims=True)
    e2x = jnp.exp(m2 - m1)
    w1 = p_sel / (1.0 + e2x)
    w2 = p_sel * e2x / (1.0 + e2x)

    oh1 = col == (i1 - N_GROUPS)
    oh2 = col == (i2 - N_GROUPS)
    oh = jnp.where(oh1 | oh2, 1.0, 0.0)
    r_i = lax.broadcasted_iota(I32, (tm, tm), 0)
    c_i = lax.broadcasted_iota(I32, (tm, tm), 1)
    earlier = jnp.where(r_i > c_i, 1.0, 0.0).astype(BF16)
    before = jnp.dot(earlier, oh.astype(BF16), preferred_element_type=F32)
    cnt = jnp.sum(oh, axis=0, keepdims=True)
    k16 = jnp.floor((cnt + (BAND - 1.0)) * (1.0 / BAND))
    k16_8 = jnp.broadcast_to(k16, (SUBLANES, LANES))
    e_r = lax.broadcasted_iota(I32, (LANES, LANES), 0)
    e_c = lax.broadcasted_iota(I32, (LANES, LANES), 1)
    lower_e = jnp.where(e_r < e_c, 1.0, 0.0).astype(BF16)
    seg_start = jnp.dot(k16_8.astype(BF16), lower_e, preferred_element_type=F32)[0:1, :] * BAND
    base = before + seg_start
    lp1 = jnp.sum(jnp.where(oh1, base, 0.0), axis=1, keepdims=True)
    lp2 = jnp.sum(jnp.where(oh2, base, 0.0), axis=1, keepdims=True)

    info = jnp.where(col == 0, lp1, jnp.where(col == 1, lp2, jnp.where(col == 2, w1, jnp.where(col == 3, w2, 0.0))))
    info_ref[...] = info
    infot_ref[...] = info.T[0:SUBLANES, :]
    k16_ref[...] = k16_8


def _router_call(h, w_r, b_r, l):
    t, d = h.shape
    tm = MOE_TM
    nt = t // tm
    return pl.pallas_call(
        _router_kernel,
        grid=(nt,),
        in_specs=[pl.BlockSpec((tm, d), lambda i: (i, 0)),
                  pl.BlockSpec((None, d, LANES), lambda i: (l, 0, 0)),
                  pl.BlockSpec((None, 1, LANES), lambda i: (l, 0, 0))],
        out_specs=[pl.BlockSpec((tm, LANES), lambda i: (i, 0)),
                   pl.BlockSpec((None, SUBLANES, tm), lambda i: (i, 0, 0)),
                   pl.BlockSpec((None, SUBLANES, LANES), lambda i: (i, 0, 0))],
        out_shape=[jax.ShapeDtypeStruct((t, LANES), F32),
                   jax.ShapeDtypeStruct((nt, SUBLANES, tm), F32),
                   jax.ShapeDtypeStruct((nt, SUBLANES, LANES), F32)],
        compiler_params=_cparams(("parallel",)),
        name="moe_router",
    )(h, w_r, b_r)


def _moe_plan(k16, n_chunks):
    k = k16[:, 0, :N_EXPERTS].astype(I32)
    bpc = MOE_TMG // BAND
    tot = jnp.sum(k, axis=0)
    padc = ((tot + bpc - 1) // bpc) * bpc
    end_e = jnp.cumsum(padc)
    start_e = end_e - padc
    gstart = start_e[None, :] + jnp.cumsum(k, axis=0) - k
    lend = jnp.cumsum(k, axis=1)
    lstart = lend - k
    b = jnp.arange(MOE_MAXB, dtype=I32)
    eb = jnp.sum((b[None, :, None] >= lend[:, None, :]).astype(I32), axis=2)
    eb = jnp.minimum(eb, N_EXPERTS - 1)
    dstb = jnp.take_along_axis(gstart, eb, axis=1) + b[None, :] - jnp.take_along_axis(lstart, eb, axis=1)
    nbands = lend[:, -1]
    dstb = jnp.where(b[None, :] < nbands[:, None], dstb, 0)
    cstart = jnp.arange(n_chunks, dtype=I32) * bpc
    cexp = jnp.sum((cstart[:, None] >= end_e[None, :]).astype(I32), axis=1)
    cexp = jnp.minimum(cexp, N_EXPERTS - 1)
    n_used = (end_e[-1] // bpc).reshape(1)
    n_tiles = k.shape[0]
    padb = _pad_bands_per_tile(n_tiles)
    total_bands = n_chunks * bpc
    gap_start = jnp.concatenate([start_e + tot, end_e[-1:]])
    gap_len = jnp.concatenate([padc - tot, total_bands - end_e[-1:]])
    pend = jnp.cumsum(gap_len)
    pstart = pend - gap_len
    j = jnp.arange(n_tiles * padb, dtype=I32)
    gj = jnp.minimum(jnp.sum((j[:, None] >= pend[None, :]).astype(I32), axis=1), N_EXPERTS)
    dstp = jnp.take(gap_start, gj) + j - jnp.take(pstart, gj)
    dstp = jnp.where(j < pend[-1], dstp, 0)
    npad = jnp.clip(pend[-1] - jnp.arange(n_tiles, dtype=I32) * padb, 0, padb)
    return dstb.reshape(-1), nbands, dstp, npad, cexp, n_used


def _pad_bands_per_tile(n_tiles):
    worst = (n_tiles * MOE_MAXR + N_EXPERTS * MOE_TMG - 2 * n_tiles * MOE_TM) // BAND
    return -(-worst // n_tiles)


def _band_copy_out(xloc, xs_ref, dstb_ref, sem, tile, b):
    src = xloc.at[pl.ds(pl.multiple_of(b * BAND, BAND), BAND), :]
    row = pl.multiple_of(dstb_ref[tile * MOE_MAXB + b] * BAND, BAND)
    return pltpu.make_async_copy(src, xs_ref.at[pl.ds(row, BAND), :], sem)


def _zero_band_copy(xloc, xs_ref, dstp_ref, sem, padb, tile, j):
    row = pl.multiple_of(dstp_ref[tile * padb + j] * BAND, BAND)
    return pltpu.make_async_copy(xloc.at[pl.ds(MOE_MAXR, BAND), :], xs_ref.at[pl.ds(row, BAND), :], sem)


def _dispatch_kernel(dstb_ref, nb_ref, dstp_ref, npad_ref, h_ref, lpt_ref, xs_ref, xloc, sem, *, padb):
    tile = pl.program_id(0)
    tm, d = h_ref.shape
    lp1 = lpt_ref[0:1, :]
    lp2 = lpt_ref[1:2, :]
    rows = lax.broadcasted_iota(I32, (MOE_MAXR + BAND, tm), 0).astype(F32)
    perm = jnp.where((rows == lp1) | (rows == lp2), 1.0, 0.0).astype(BF16)
    for c0 in range(0, d, MOE_COLS):
        xloc[:, c0:c0 + MOE_COLS] = jnp.dot(
            perm, h_ref[:, c0:c0 + MOE_COLS], preferred_element_type=F32).astype(BF16)
    n = nb_ref[tile]
    n_zero = npad_ref[tile]

    def start(b, carry):
        _band_copy_out(xloc, xs_ref, dstb_ref, sem, tile, b).start()
        return carry

    def wait(b, carry):
        _band_copy_out(xloc, xs_ref, dstb_ref, sem, tile, b).wait()
        return carry

    def start_zero(j, carry):
        _zero_band_copy(xloc, xs_ref, dstp_ref, sem, padb, tile, j).start()
        return carry

    def wait_zero(j, carry):
        _zero_band_copy(xloc, xs_ref, dstp_ref, sem, padb, tile, j).wait()
        return carry

    lax.fori_loop(0, n, start, 0)
    lax.fori_loop(0, n_zero, start_zero, 0)
    lax.fori_loop(0, n, wait, 0)
    lax.fori_loop(0, n_zero, wait_zero, 0)


def _dispatch_call(h, infot, dstb, nbands, dstp, npad, n_rows):
    t, d = h.shape
    nt = t // MOE_TM
    return pl.pallas_call(
        functools.partial(_dispatch_kernel, padb=_pad_bands_per_tile(nt)),
        grid_spec=pltpu.PrefetchScalarGridSpec(
            num_scalar_prefetch=4,
            grid=(nt,),
            in_specs=[pl.BlockSpec((MOE_TM, d), lambda i, *_: (i, 0)),
                      pl.BlockSpec((None, SUBLANES, MOE_TM), lambda i, *_: (i, 0, 0))],
            out_specs=pl.BlockSpec(memory_space=pl.ANY),
            scratch_shapes=[pltpu.VMEM((MOE_MAXR + BAND, d), BF16), pltpu.SemaphoreType.DMA(())]),
        out_shape=jax.ShapeDtypeStruct((n_rows, d), BF16),
        compiler_params=_cparams(("arbitrary",)),
        name="moe_dispatch",
    )(dstb, nbands, dstp, npad, h, infot)


def _experts_kernel(cexp_ref, nused_ref, x_ref, wgu_ref, wd_ref, y_ref):
    del cexp_ref
    f = wd_ref.shape[0]
    c = pl.program_id(0)

    @pl.when(c < nused_ref[0])
    def _():
        half = x_ref.shape[0] // 2
        parts = [slice(0, half), slice(half, 2 * half)]
        gus = [jnp.dot(x_ref[rows, :], wgu_ref[...], preferred_element_type=F32) for rows in parts]
        for rows, gu in zip(parts, gus):
            g, u = gu[:, 0:f], gu[:, f:]
            act = ((g * jax.nn.sigmoid(g)) * u).astype(BF16)
            y_ref[rows, :] = jnp.dot(act, wd_ref[...], preferred_element_type=F32).astype(y_ref.dtype)

    @pl.when(c >= nused_ref[0])
    def _():
        y_ref[...] = jnp.zeros_like(y_ref)


def _experts_call(xs, wgu, wd, l, cexp, n_used):
    n_rows, d = xs.shape
    f2 = wgu.shape[3]
    f = wd.shape[2]
    n_chunks = n_rows // MOE_TMG
    return pl.pallas_call(
        _experts_kernel,
        grid_spec=pltpu.PrefetchScalarGridSpec(
            num_scalar_prefetch=2,
            grid=(n_chunks,),
            in_specs=[pl.BlockSpec((MOE_TMG, d), lambda c, ce, nu: (jnp.minimum(c, nu[0] - 1), 0)),
                      pl.BlockSpec((None, None, d, f2), lambda c, ce, nu: (l, ce[c], 0, 0)),
                      pl.BlockSpec((None, None, f, d), lambda c, ce, nu: (l, ce[c], 0, 0))],
            out_specs=pl.BlockSpec((MOE_TMG, d), lambda c, ce, nu: (c, 0))),
        out_shape=jax.ShapeDtypeStruct((n_rows, d), BF16),
        compiler_params=_cparams(("arbitrary",)),
        name="moe_experts",
    )(cexp, n_used, xs, wgu, wd)


def _band_copy_in(y_ref, yloc, dstb_ref, sem, tile, b):
    row = pl.multiple_of(dstb_ref[tile * MOE_MAXB + b] * BAND, BAND)
    dst = yloc.at[pl.ds(pl.multiple_of(b * BAND, BAND), BAND), :]
    return pltpu.make_async_copy(y_ref.at[pl.ds(row, BAND), :], dst, sem)


def _combine_kernel(dstb_ref, nb_ref, y_ref, info_ref, x_ref, g_ref, o_ref, yloc, sem):
    tile = pl.program_id(0)
    tm, d = x_ref.shape
    n = nb_ref[tile]

    def start(b, carry):
        _band_copy_in(y_ref, yloc, dstb_ref, sem, tile, b).start()
        return carry

    def wait(b, carry):
        _band_copy_in(y_ref, yloc, dstb_ref, sem, tile, b).wait()
        return carry

    def clear(b, carry):
        yloc[pl.ds(pl.multiple_of(b * BAND, BAND), BAND), :] = jnp.zeros((BAND, d), yloc.dtype)
        return carry

    lax.fori_loop(0, n, start, 0)
    lax.fori_loop(n, MOE_MAXB, clear, 0)
    info = info_ref[...]
    lp1, lp2, w1, w2 = info[:, 0:1], info[:, 1:2], info[:, 2:3], info[:, 3:4]
    cols = lax.broadcasted_iota(I32, (tm, MOE_MAXR), 1).astype(F32)
    pw = (jnp.where(cols == lp1, w1, 0.0) + jnp.where(cols == lp2, w2, 0.0)).astype(BF16)
    lax.fori_loop(0, n, wait, 0)
    for c0 in range(0, d, MOE_COLS):
        sl = slice(c0, c0 + MOE_COLS)
        acc = jnp.dot(pw, yloc[:, sl], preferred_element_type=F32)
        o_ref[:, sl] = x_ref[:, sl] + g_ref[:, sl] * acc


def _combine_call(y, info, x, mod, l, gate_blk, dstb, nbands):
    t, d = x.shape
    nt = t // MOE_TM
    return pl.pallas_call(
        _combine_kernel,
        grid_spec=pltpu.PrefetchScalarGridSpec(
            num_scalar_prefetch=2,
            grid=(nt,),
            in_specs=[pl.BlockSpec(memory_space=pl.ANY),
                      pl.BlockSpec((MOE_TM, LANES), lambda i, *_: (i, 0)),
                      pl.BlockSpec((MOE_TM, d), lambda i, *_: (i, 0)),
                      pl.BlockSpec((None, 1, d), lambda i, *_: (l, 0, gate_blk))],
            out_specs=pl.BlockSpec((MOE_TM, d), lambda i, *_: (i, 0)),
            scratch_shapes=[pltpu.VMEM((MOE_MAXR, d), BF16), pltpu.SemaphoreType.DMA(())]),
        out_shape=jax.ShapeDtypeStruct((t, d), F32),
        compiler_params=_cparams(("arbitrary",)),
        name="moe_combine",
    )(dstb, nbands, y, info, x, mod)


def kernel(x, c, w_mod, mod_table, norm1_g, w_in, ssm_a_re, ssm_a_im, ssm_log_dt, ssm_b_re, ssm_b_im, ssm_c_re, ssm_c_im, ssm_d, w_glu, w_br_ssm, conv_w, w_br_conv, w_o, norm2_g, w_router_group, b_router_group, w_router_expert, b_router_expert, w_exp_gate, w_exp_up, w_exp_down, final_g):
    bsz, seq, d = x.shape
    depth = mod_table.shape[0]
    h_ssm = ssm_d.shape[1]
    h_conv = conv_w.shape[2]
    n_exp, _, d_expert = w_exp_gate.shape[1:]
    assert bsz == 1 and seq % SSM_TT == 0 and h_ssm % SSM_CB == 0 and seq % MOE_TM == 0
    assert ssm_a_re.shape[1:] == (h_ssm // SSM_GROUP, SSM_STATE) and n_exp == N_EXPERTS
    assert d % MOE_COLS == 0
    u_col, b_col, c_col, v_col = 0, h_ssm, h_ssm + h_conv, h_ssm + 2 * h_conv
    gs_col = h_ssm + 3 * h_conv
    gc_col = gs_col + d
    sh1, sc1, g1, sh2, sc2, g2 = range(N_MOD)

    w_in_b = w_in.astype(BF16)
    w_glu_b = w_glu.astype(BF16)
    w_brs_b = w_br_ssm.astype(BF16)
    w_brc_b = w_br_conv.astype(BF16)
    w_o_b = w_o.astype(BF16)
    wgu_b = jnp.concatenate([w_exp_gate, w_exp_up], axis=3).astype(BF16)
    wd_b = w_exp_down.astype(BF16)
    pad = LANES - N_GROUPS - N_EXPERTS
    w_r = jnp.concatenate([w_router_group, w_router_expert, jnp.zeros((depth, d, pad), F32)], axis=2).astype(BF16)
    b_r = jnp.concatenate([b_router_group, b_router_expert, jnp.zeros((depth, pad), F32)], axis=1)[:, None, :]
    s5_kc, s5_pc, s5_qc, ltab = _ssm_tables(ssm_a_re, ssm_a_im, ssm_log_dt, ssm_b_re, ssm_b_im,
                                           ssm_c_re, ssm_c_im)

    n_tiles = seq // MOE_TM
    n_rows = n_tiles * MOE_MAXR + N_EXPERTS * MOE_TMG
    n_chunks = n_rows // MOE_TMG

    xt = x.reshape(seq, d)
    mod = _mod_call(c, w_mod, mod_table)[:, None, :]
    norm1_g, norm2_g, ssm_d = norm1_g[:, None, :], norm2_g[:, None, :], ssm_d[:, None, :]

    for l in range(depth):
        h1 = _norm_mod_call(xt, norm1_g, mod, l, sc1, sh1)
        proj = _mm_call(h1, w_in_b, l, F32, 1024, 512, "in_proj")
        y_pre = _ssm_call(proj, u_col, h_ssm, s5_kc, s5_pc, s5_qc, ltab, ssm_d, l)
        ys = _glu_call(y_pre, w_glu_b, l)
        yc = _conv_call(proj, conv_w, l, b_col, c_col, v_col, h_conv)
        merged = _merge_call(ys, yc, w_brs_b, w_brc_b, l, proj, gs_col, gc_col)
        xt = _resid_call(merged, w_o_b, l, xt, mod, g1, 1024, 512, "out_proj")

        h2 = _norm_mod_call(xt, norm2_g, mod, l, sc2, sh2)
        info, infot, k16 = _router_call(h2, w_r, b_r, l)
        dstb, nbands, dstp, npad, cexp, n_used = _moe_plan(k16, n_chunks)
        xs = _dispatch_call(h2, infot, dstb, nbands, dstp, npad, n_rows)
        ye = _experts_call(xs, wgu_b, wd_b, l, cexp, n_used)
        xt = _combine_call(ye, info, xt, mod, l, g2, dstb, nbands)

    out = _norm_call(xt, final_g[None, :])
    return out.reshape(bsz, seq, d)
```

```python
import functools
import math

import jax
import jax.numpy as jnp
from jax import lax
from jax.experimental import pallas as pl
from jax.experimental.pallas import tpu as pltpu

F32 = jnp.float32
BF16 = jnp.bfloat16
I32 = jnp.int32

EPS = 1e-6
SSM_GROUP = 16
SSM_STATE = 64
CONV_WIDTH = 3
N_GROUPS = 4
EXPERTS_PER_GROUP = 4
N_EXPERTS = N_GROUPS * EXPERTS_PER_GROUP
N_MOD = 6

LANES = 128
SUBLANES = 8
VMEM_LIMIT_BYTES = 58 * 1024 * 1024

SSM_LC = 16
SSM_LA = 4
SSM_LB = 4
SSM_S2 = 16
SSM_K2 = 16
SSM_NK = SSM_S2 * SSM_K2
SSM_TT = SSM_NK * SSM_LC
SSM_PITCH = SSM_S2 * SSM_LC + SUBLANES
SSM_CB = 256
SSM_GB = SSM_CB // SSM_GROUP
SSM_NS = SSM_GB * SSM_STATE
LT_ROWS = 24
SSM_NPOW = 32

MOE_TM = 512
BAND = 2 * SUBLANES
MOE_MAXR = 2 * MOE_TM + N_EXPERTS * BAND
MOE_MAXB = MOE_MAXR // BAND
MOE_TMG = 512
MOE_COLS = 1024


def _cparams(sem):
    return pltpu.CompilerParams(dimension_semantics=sem, vmem_limit_bytes=VMEM_LIMIT_BYTES)


def _mod_kernel(c_ref, w_ref, t_ref, o_ref):
    c = c_ref[...]
    s = c * jax.nn.sigmoid(c)
    shared = jnp.sum(s * w_ref[...], axis=0, keepdims=True)
    o_ref[...] = t_ref[...] + shared


def _mod_call(c, w_mod, mod_table):
    d, n = w_mod.shape
    depth = mod_table.shape[0]
    tn = 512
    return pl.pallas_call(
        _mod_kernel,
        grid=(n // tn,),
        in_specs=[pl.BlockSpec((d, 1), lambda j: (0, 0)),
                  pl.BlockSpec((d, tn), lambda j: (0, j)),
                  pl.BlockSpec((depth, tn), lambda j: (0, j))],
        out_specs=pl.BlockSpec((depth, tn), lambda j: (0, j)),
        out_shape=jax.ShapeDtypeStruct((depth, n), F32),
        compiler_params=_cparams(("arbitrary",)),
        name="adaln_mod",
    )(c.reshape(d, 1), w_mod, mod_table)


def _norm_mod_kernel(x_ref, g_ref, sc_ref, sh_ref, o_ref):
    x = x_ref[...]
    y = x * lax.rsqrt(jnp.mean(x * x, axis=-1, keepdims=True) + EPS)
    o_ref[...] = ((y * g_ref[...]) * (1.0 + sc_ref[...]) + sh_ref[...]).astype(o_ref.dtype)


def _norm_kernel(x_ref, g_ref, o_ref):
    x = x_ref[...]
    y = x * lax.rsqrt(jnp.mean(x * x, axis=-1, keepdims=True) + EPS)
    o_ref[...] = (y * g_ref[...]).astype(o_ref.dtype)


def _norm_mod_call(x, gains, mod, l, sc_blk, sh_blk, tm=256):
    t, d = x.shape
    return pl.pallas_call(
        _norm_mod_kernel,
        grid=(t // tm,),
        in_specs=[pl.BlockSpec((tm, d), lambda i: (i, 0)),
                  pl.BlockSpec((None, 1, d), lambda i: (l, 0, 0)),
                  pl.BlockSpec((None, 1, d), lambda i: (l, 0, sc_blk)),
                  pl.BlockSpec((None, 1, d), lambda i: (l, 0, sh_blk))],
        out_specs=pl.BlockSpec((tm, d), lambda i: (i, 0)),
        out_shape=jax.ShapeDtypeStruct((t, d), BF16),
        compiler_params=_cparams(("parallel",)),
        name="norm_mod",
    )(x, gains, mod, mod)


def _norm_call(x, g, tm=256):
    t, d = x.shape
    return pl.pallas_call(
        _norm_kernel,
        grid=(t // tm,),
        in_specs=[pl.BlockSpec((tm, d), lambda i: (i, 0)), pl.BlockSpec((1, d), lambda i: (0, 0))],
        out_specs=pl.BlockSpec((tm, d), lambda i: (i, 0)),
        out_shape=jax.ShapeDtypeStruct((t, d), F32),
        compiler_params=_cparams(("parallel",)),
        name="final_norm",
    )(x, g)


def _wspec(l, k, tn):
    return pl.BlockSpec((None, k, tn), lambda i, j: (l, 0, j))


def _mm_kernel(a_ref, w_ref, o_ref):
    o_ref[...] = jnp.dot(a_ref[...], w_ref[...], preferred_element_type=F32).astype(o_ref.dtype)


def _mm_call(a, w, l, out_dtype, tm, tn, name):
    t, k = a.shape
    n = w.shape[2]
    return pl.pallas_call(
        _mm_kernel,
        grid=(t // tm, n // tn),
        in_specs=[pl.BlockSpec((tm, k), lambda i, j: (i, 0)), _wspec(l, k, tn)],
        out_specs=pl.BlockSpec((tm, tn), lambda i, j: (i, j)),
        out_shape=jax.ShapeDtypeStruct((t, n), out_dtype),
        compiler_params=_cparams(("parallel", "arbitrary")),
        name=name,
    )(a, w)


def _glu_kernel(a_ref, y_ref, w_ref, o_ref):
    acc = jnp.dot(a_ref[...], w_ref[...], preferred_element_type=F32)
    o_ref[...] = (y_ref[...].astype(F32) * jax.nn.sigmoid(acc)).astype(o_ref.dtype)


def _glu_call(y, w, l, tm=1024, tn=512):
    t, k = y.shape
    n = w.shape[2]
    return pl.pallas_call(
        _glu_kernel,
        grid=(t // tm, n // tn),
        in_specs=[pl.BlockSpec((tm, k), lambda i, j: (i, 0)),
                  pl.BlockSpec((tm, tn), lambda i, j: (i, j)),
                  _wspec(l, k, tn)],
        out_specs=pl.BlockSpec((tm, tn), lambda i, j: (i, j)),
        out_shape=jax.ShapeDtypeStruct((t, n), BF16),
        compiler_params=_cparams(("parallel", "arbitrary")),
        name="s5_glu",
    )(y, y, w)


def _merge_kernel(ys_ref, yc_ref, ws_ref, wc_ref, gs_ref, gc_ref, o_ref):
    y_ssm = jnp.dot(ys_ref[...], ws_ref[...], preferred_element_type=F32)
    y_conv = jnp.dot(yc_ref[...], wc_ref[...], preferred_element_type=F32)
    merged = jax.nn.sigmoid(gs_ref[...]) * y_ssm + jax.nn.sigmoid(gc_ref[...]) * y_conv
    o_ref[...] = merged.astype(o_ref.dtype)


def _merge_call(ys, yc, ws, wc, l, proj, gs_col, gc_col, tm=1024, tn=512):
    t, k = ys.shape
    n = ws.shape[2]
    gs_blk, gc_blk = gs_col // tn, gc_col // tn
    return pl.pallas_call(
        _merge_kernel,
        grid=(t // tm, n // tn),
        in_specs=[pl.BlockSpec((tm, k), lambda i, j: (i, 0)),
                  pl.BlockSpec((tm, k), lambda i, j: (i, 0)),
                  _wspec(l, k, tn),
                  _wspec(l, k, tn),
                  pl.BlockSpec((tm, tn), lambda i, j: (i, gs_blk + j)),
                  pl.BlockSpec((tm, tn), lambda i, j: (i, gc_blk + j))],
        out_specs=pl.BlockSpec((tm, tn), lambda i, j: (i, j)),
        out_shape=jax.ShapeDtypeStruct((t, n), BF16),
        compiler_params=_cparams(("parallel", "arbitrary")),
        name="branch_merge",
    )(ys, yc, ws, wc, proj, proj)


def _resid_kernel(a_ref, w_ref, x_ref, g_ref, o_ref):
    acc = jnp.dot(a_ref[...], w_ref[...], preferred_element_type=F32)
    o_ref[...] = x_ref[...] + g_ref[...] * acc


def _resid_call(a, w, l, x, mod, gate_blk, tm, tn, name):
    t, k = a.shape
    n = w.shape[2]
    gb = gate_blk * (n // tn)
    return pl.pallas_call(
        _resid_kernel,
        grid=(t // tm, n // tn),
        in_specs=[pl.BlockSpec((tm, k), lambda i, j: (i, 0)),
                  _wspec(l, k, tn),
                  pl.BlockSpec((tm, tn), lambda i, j: (i, j)),
                  pl.BlockSpec((None, 1, tn), lambda i, j: (l, 0, gb + j))],
        out_specs=pl.BlockSpec((tm, tn), lambda i, j: (i, j)),
        out_shape=jax.ShapeDtypeStruct((t, n), F32),
        compiler_params=_cparams(("parallel", "arbitrary")),
        name=name,
    )(a, w, x, mod)


def _conv_kernel(b_ref, c_ref, v_ref, ch_ref, vh_ref, w_ref, o_ref, z_scr):
    tm = b_ref.shape[0]
    halo = SUBLANES
    z = c_ref[...] * v_ref[...]
    zh = jnp.where(pl.program_id(0) == 0, 0.0, ch_ref[...] * vh_ref[...])
    z_scr[0:halo, :] = zh
    z_scr[halo:, :] = z
    w = w_ref[...]
    conv = w[2:3, :] * z
    for k in range(CONV_WIDTH - 1):
        conv = conv + w[k:k + 1, :] * z_scr[pl.ds(halo - (CONV_WIDTH - 1 - k), tm), :]
    o_ref[...] = (b_ref[...] * conv).astype(o_ref.dtype)


def _conv_call(proj, conv_w, l, b_col, c_col, v_col, width, tm=512, tc=512):
    t = proj.shape[0]
    bb, cb, vb = b_col // tc, c_col // tc, v_col // tc
    hb = tm // SUBLANES

    def halo_map(off):
        return lambda i, j: (jnp.maximum(i * hb - 1, 0), off + j)

    return pl.pallas_call(
        _conv_kernel,
        grid=(t // tm, width // tc),
        in_specs=[pl.BlockSpec((tm, tc), lambda i, j: (i, bb + j)),
                  pl.BlockSpec((tm, tc), lambda i, j: (i, cb + j)),
                  pl.BlockSpec((tm, tc), lambda i, j: (i, vb + j)),
                  pl.BlockSpec((SUBLANES, tc), halo_map(cb)),
                  pl.BlockSpec((SUBLANES, tc), halo_map(vb)),
                  pl.BlockSpec((None, CONV_WIDTH, tc), lambda i, j: (l, 0, j))],
        out_specs=pl.BlockSpec((tm, tc), lambda i, j: (i, j)),
        out_shape=jax.ShapeDtypeStruct((t, width), BF16),
        scratch_shapes=[pltpu.VMEM((tm + SUBLANES, tc), F32)],
        compiler_params=_cparams(("parallel", "parallel")),
        name="short_conv",
    )(proj, proj, proj, proj, proj, conv_w)


def _cmul_row(xr, xi, lr, li):
    return xr * lr - xi * li, xr * li + xi * lr


def _gelu_tanh(x):
    return 0.5 * x * (1.0 + jnp.tanh(math.sqrt(2.0 / math.pi) * (x + 0.044715 * (x * x * x))))


def _expand_block_diag(dst, spread_rows, row_shift, col_shift, chunk):
    n_r, n_c = dst.shape
    gmask = SSM_GB - 1
    for r0 in range(0, n_r, chunk):
        spread = spread_rows(r0, chunk)
        rg = (lax.broadcasted_iota(I32, (chunk, n_c), 0) + r0) >> row_shift
        cg = lax.broadcasted_iota(I32, (chunk, n_c), 1) >> col_shift
        keep = (rg & gmask) == (cg & gmask)
        dst[r0:r0 + chunk, :] = jnp.where(keep, spread, 0.0).astype(dst.dtype)


def _ssm_build_tables(pc_ref, qc_ref, kc_ref, p_scr, q_scr, k_scr):
    ns2 = 2 * SSM_NS
    p_state = SSM_STATE
    lp, lc = int(math.log2(p_state)), int(math.log2(SSM_GROUP))
    lns = int(math.log2(SSM_NS))
    r = lax.broadcasted_iota(I32, (LANES, ns2), 0)
    c = lax.broadcasted_iota(I32, (LANES, ns2), 1)
    tile_p = jnp.where(((r >> lp) == (c >> lns)) & ((r & (p_state - 1)) == (c & (p_state - 1))),
                       1.0, 0.0).astype(BF16)

    def p_rows(r0, chunk):
        return jnp.dot(pc_ref[r0:r0 + chunk, :], tile_p, preferred_element_type=F32)

    _expand_block_diag(p_scr, p_rows, lc, lp, 256)
    r = lax.broadcasted_iota(I32, (ns2, LANES), 0)
    c = lax.broadcasted_iota(I32, (ns2, LANES), 1)
    tile_q = jnp.where(((c >> lp) == (r >> lns)) & ((c & (p_state - 1)) == (r & (p_state - 1))),
                       1.0, 0.0).astype(BF16)

    def q_rows(r0, chunk):
        return jnp.dot(tile_q[r0:r0 + chunk, :], qc_ref[...], preferred_element_type=F32)

    _expand_block_diag(q_scr, q_rows, lp, lc, 512)
    r = lax.broadcasted_iota(I32, (LANES, SSM_CB), 0)
    c = lax.broadcasted_iota(I32, (LANES, SSM_CB), 1)
    tile_k = jnp.where(r == (c & (SSM_GROUP - 1)), 1.0, 0.0).astype(BF16)

    def k_rows(r0, chunk):
        return jnp.dot(kc_ref[r0:r0 + chunk, :], tile_k, preferred_element_type=F32)

    _expand_block_diag(k_scr, k_rows, lc, lc, 1024)


def _ssm_kernel(u0_ref, u1_ref, kc_ref, pc_ref, qc_ref, lt_ref, d_ref, o_ref,
                ub2, y2, bx, s_scr, ha, cs, carry, yo0, yo1, kall_ref, p_ref, q_ref, up0, up1):
    ns = SSM_NS
    cb = SSM_CB
    rows = SSM_K2
    ksteps = SSM_S2 * SSM_LC
    kstride = SSM_PITCH
    u_halves = (up0, up1)
    yo = (yo0, yo1)

    for src, dst in ((u0_ref, up0), (u1_ref, up1)):
        for k2 in range(SSM_K2):
            dst[k2 * kstride:k2 * kstride + ksteps, :] = src[k2 * ksteps:(k2 + 1) * ksteps, :]

    @pl.when(pl.program_id(1) == 0)
    def _():
        carry[...] = jnp.zeros_like(carry)
        _ssm_build_tables(pc_ref, qc_ref, kc_ref, p_ref, q_ref, kall_ref)

    def lt(row):
        return lt_ref[row:row + 1, 0:ns], lt_ref[row:row + 1, ns:2 * ns]

    for s in range(SSM_LC):
        for s2 in range(SSM_S2):
            idx = pl.ds(s2 * SSM_LC + s, rows, stride=kstride)
            for hf, u_ref in enumerate(u_halves):
                c0 = s * cb + hf * LANES
                ub2[s2 * rows:(s2 + 1) * rows, c0:c0 + LANES] = u_ref[idx, :].astype(BF16)

    for sp in range(SSM_LC):
        y2[:, sp * cb:(sp + 1) * cb] = jnp.dot(
            ub2[:, 0:(sp + 1) * cb], kall_ref[(SSM_LC - 1 - sp) * cb:, :],
            preferred_element_type=F32)

    s_scr[...] = jnp.dot(ub2[:, (SSM_LA - 1) * SSM_LB * cb:], p_ref[...], preferred_element_type=F32)
    for a in range(SSM_LA - 1):
        bx[...] = jnp.dot(ub2[:, a * SSM_LB * cb:(a + 1) * SSM_LB * cb], p_ref[...],
                          preferred_element_type=F32)
        lr, li = lt(SSM_LA - 2 - a)
        for s2 in range(SSM_S2):
            sl = slice(s2 * rows, (s2 + 1) * rows)
            pr, pi = _cmul_row(bx[sl, 0:ns], bx[sl, ns:], lr, li)
            s_scr[sl, 0:ns] += pr
            s_scr[sl, ns:] += pi

    lr, li = lt(4 + 1)
    for s2 in range(1, SSM_S2):
        pv = slice((s2 - 1) * rows, s2 * rows)
        sl = slice(s2 * rows, (s2 + 1) * rows)
        pr, pi = _cmul_row(s_scr[pv, 0:ns], s_scr[pv, ns:], lr, li)
        s_scr[sl, 0:ns] += pr
        s_scr[sl, ns:] += pi

    lr, li = lt(3)
    last = slice((SSM_S2 - 1) * rows, SSM_S2 * rows)
    zr, zi = s_scr[last, 0:ns], s_scr[last, ns:]
    cr, ci = carry[0:1, 0:ns], carry[0:1, ns:]
    for k2 in range(SSM_K2):
        cs[k2:k2 + 1, 0:ns] = cr
        cs[k2:k2 + 1, ns:] = ci
        pr, pi = _cmul_row(cr, ci, lr, li)
        cr, ci = pr + zr[k2:k2 + 1, :], pi + zi[k2:k2 + 1, :]
    carry[0:1, 0:ns] = cr
    carry[0:1, ns:] = ci

    for a in range(SSM_LA):
        for s2 in range(SSM_S2):
            sl = slice(s2 * rows, (s2 + 1) * rows)
            hr, hi = cs[:, 0:ns], cs[:, ns:]
            if s2 > 0:
                pv = slice((s2 - 1) * rows, s2 * rows)
                lr, li = lt(4 + s2)
                pr, pi = _cmul_row(hr, hi, lr, li)
                hr, hi = s_scr[pv, 0:ns] + pr, s_scr[pv, ns:] + pi
            if a > 0:
                lr, li = lt(a - 1)
                hr, hi = _cmul_row(hr, hi, lr, li)
            ha[sl, 0:ns] = hr.astype(BF16)
            ha[sl, ns:] = hi.astype(BF16)
        wa = SSM_LB * cb
        y2[:, a * wa:(a + 1) * wa] += jnp.dot(ha[...], q_ref[...], preferred_element_type=F32)

    d = d_ref[...]
    for s in range(SSM_LC):
        for s2 in range(SSM_S2):
            idx = pl.ds(s2 * SSM_LC + s, rows, stride=kstride)
            for hf, u_ref in enumerate(u_halves):
                c0 = s * cb + hf * LANES
                y = (y2[s2 * rows:(s2 + 1) * rows, c0:c0 + LANES]
                     + d[:, hf * LANES:(hf + 1) * LANES] * u_ref[idx, :])
                yo[hf][idx, :] = _gelu_tanh(y)
    for hf in range(len(u_halves)):
        for k2 in range(SSM_K2):
            o_ref[k2 * ksteps:(k2 + 1) * ksteps, hf * LANES:(hf + 1) * LANES] = (
                yo[hf][k2 * kstride:k2 * kstride + ksteps, :].astype(o_ref.dtype))


def _ssm_call(proj, u_col, h_ssm, kc, pc, qc, ltab, d_skip, l):
    t = proj.shape[0]
    nb = h_ssm // SSM_CB
    ub = u_col // SSM_CB
    ns2 = 2 * SSM_NS

    def tab(r, c):
        return pl.BlockSpec((None, None, r, c), lambda b, i: (l, b, 0, 0))

    return pl.pallas_call(
        _ssm_kernel,
        grid=(nb, t // SSM_TT),
        in_specs=[pl.BlockSpec((SSM_TT, LANES), lambda b, i: (i, 2 * (ub + b))),
                  pl.BlockSpec((SSM_TT, LANES), lambda b, i: (i, 2 * (ub + b) + 1)),
                  tab(SSM_LC * SSM_CB, LANES),
                  tab(SSM_LB * SSM_CB, LANES),
                  tab(LANES, SSM_LB * SSM_CB),
                  tab(LT_ROWS, ns2),
                  pl.BlockSpec((None, 1, SSM_CB), lambda b, i: (l, 0, b))],
        out_specs=pl.BlockSpec((SSM_TT, SSM_CB), lambda b, i: (i, b)),
        out_shape=jax.ShapeDtypeStruct((t, h_ssm), BF16),
        scratch_shapes=[pltpu.VMEM((SSM_NK, SSM_LC * SSM_CB), BF16),
                        pltpu.VMEM((SSM_NK, SSM_LC * SSM_CB), F32),
                        pltpu.VMEM((SSM_NK, ns2), F32),
                        pltpu.VMEM((SSM_NK, ns2), F32),
                        pltpu.VMEM((SSM_NK, ns2), BF16),
                        pltpu.VMEM((SSM_K2, ns2), F32),
                        pltpu.VMEM((SUBLANES, ns2), F32),
                        pltpu.VMEM((SSM_K2 * SSM_PITCH, LANES), F32),
                        pltpu.VMEM((SSM_K2 * SSM_PITCH, LANES), F32),
                        pltpu.VMEM((SSM_LC * SSM_CB, SSM_CB), BF16),
                        pltpu.VMEM((SSM_LB * SSM_CB, ns2), BF16),
                        pltpu.VMEM((ns2, SSM_LB * SSM_CB), BF16),
                        pltpu.VMEM((SSM_K2 * SSM_PITCH, LANES), F32),
                        pltpu.VMEM((SSM_K2 * SSM_PITCH, LANES), F32)],
        compiler_params=_cparams(("arbitrary", "arbitrary")),
        name="s5_scan",
    )(proj, proj, kc, pc, qc, ltab, d_skip)


def _ssm_tables(a_re, a_im, log_dt, b_re, b_im, c_re, c_im):
    assert 2 * a_re.shape[2] == LANES
    nl, g, p = a_re.shape
    gc = SSM_GROUP
    gb = SSM_GB
    nb = g // gb
    hi = lax.Precision.HIGHEST
    delta = jnp.exp(log_dt)[..., None]
    mag = jnp.exp(delta * a_re)
    ang = delta * a_im
    lam = (mag * jnp.cos(ang), mag * jnp.sin(ang))
    den = a_re * a_re + a_im * a_im
    zr, zi = lam[0] - 1.0, lam[1]
    coef_re = (zr * a_re + zi * a_im) / den
    coef_im = (zi * a_re - zr * a_im) / den
    bt_re = coef_re[..., None] * b_re - coef_im[..., None] * b_im
    bt_im = coef_re[..., None] * b_im + coef_im[..., None] * b_re

    def cmul(x, y):
        return x[0] * y[0] - x[1] * y[1], x[0] * y[1] + x[1] * y[0]

    def power_table(base):
        tab = (jnp.stack([jnp.ones_like(base[0]), base[0]]), jnp.stack([jnp.zeros_like(base[1]), base[1]]))
        step = cmul(base, base)
        while tab[0].shape[0] < SSM_NPOW:
            nxt = cmul(tab, (step[0][None], step[1][None]))
            tab = (jnp.concatenate([tab[0], nxt[0]]), jnp.concatenate([tab[1], nxt[1]]))
            step = cmul(step, step)
        return tab

    pw = power_table(lam)
    pw16 = power_table((pw[0][SSM_LC], pw[1][SSM_LC]))

    wr = pw[0][:SSM_LC, ..., None] * bt_re[None] - pw[1][:SSM_LC, ..., None] * bt_im[None]
    wi = pw[0][:SSM_LC, ..., None] * bt_im[None] + pw[1][:SSM_LC, ..., None] * bt_re[None]

    k = (jnp.einsum("lgop,tlgpi->tlgio", c_re, wr, precision=hi)
         - jnp.einsum("lgop,tlgpi->tlgio", c_im, wi, precision=hi))
    k = k[::-1].reshape(SSM_LC, nl, nb, gb, gc, gc).transpose(1, 2, 0, 3, 4, 5)
    kc = k.reshape(nl, nb, SSM_LC * SSM_CB, gc)
    kc = jnp.pad(kc, ((0, 0), (0, 0), (0, 0), (0, LANES - gc))).astype(BF16)

    def p_rows(w):
        return w[SSM_LB - 1::-1].reshape(SSM_LB, nl, nb, gb, p, gc).transpose(1, 2, 0, 3, 5, 4)

    pc = jnp.stack([p_rows(wr), p_rows(wi)], axis=5)
    pc = pc.reshape(nl, nb, SSM_LB * SSM_CB, 2 * p).astype(BF16)

    qw = (pw[0][1:SSM_LB + 1, :, :, None, :], pw[1][1:SSM_LB + 1, :, :, None, :])
    dr = c_re[None] * qw[0] - c_im[None] * qw[1]
    di = c_re[None] * qw[1] + c_im[None] * qw[0]

    def q_cols(w):
        return w.reshape(SSM_LB, nl, nb, gb, gc, p).transpose(1, 2, 5, 0, 3, 4)

    qc = jnp.stack([q_cols(dr), q_cols(-di)], axis=2)
    qc = qc.reshape(nl, nb, 2 * p, SSM_LB * SSM_CB).astype(BF16)

    def lrows(part):
        r = jnp.concatenate([pw[part][SSM_LB:SSM_LC:SSM_LB], pw16[part][SSM_S2:SSM_S2 + 1],
                             pw16[part][:SSM_S2]])
        r = jnp.concatenate([r, jnp.zeros((LT_ROWS - r.shape[0],) + r.shape[1:], F32)])
        return r.reshape(LT_ROWS, nl, nb, SSM_NS).transpose(1, 2, 0, 3)

    ltab = jnp.concatenate([lrows(0), lrows(1)], axis=3)
    return kc, pc, qc, ltab


def _router_kernel(x_ref, gain_ref, sc_ref, sh_ref, w_ref, b_ref, h_ref, info_ref, infot_ref, k16_ref):
    tm = x_ref.shape[0]
    x = x_ref[...]
    y = x * lax.rsqrt(jnp.mean(x * x, axis=-1, keepdims=True) + EPS)
    h = ((y * gain_ref[...]) * (1.0 + sc_ref[...]) + sh_ref[...]).astype(h_ref.dtype)
    h_ref[...] = h
    lg = jnp.dot(h, w_ref[...], preferred_element_type=F32) + b_ref[...]
    col = lax.broadcasted_iota(I32, lg.shape, 1)
    big = jnp.int32(LANES)
    neg = -jnp.inf
    is_g = col < N_GROUPS
    lgg = jnp.where(is_g, lg, neg)
    gmax = jnp.max(lgg, axis=1, keepdims=True)
    gsel = jnp.min(jnp.where(is_g & (lgg == gmax), col, big), axis=1, keepdims=True)
    denom = jnp.sum(jnp.where(is_g, jnp.exp(lgg - gmax), 0.0), axis=1, keepdims=True)
    p_sel = 1.0 / denom
    ecol = col - N_GROUPS
    egrp = lax.shift_right_arithmetic(ecol, jnp.int32(int(math.log2(EXPERTS_PER_GROUP))))
    in_grp = (ecol >= 0) & (ecol < N_EXPERTS) & (egrp == gsel)
    le = jnp.where(in_grp, lg, neg)
    m1 = jnp.max(le, axis=1, keepdims=True)
    i1 = jnp.min(jnp.where(in_grp & (le == m1), col, big), axis=1, keepdims=True)
    rest = in_grp & (col != i1)
    le2 = jnp.where(rest, lg, neg)
    m2 = jnp.max(le2, axis=1, keepdims=True)
    i2 = jnp.min(jnp.where(rest & (le2 == m2), col, big), axis=1, keepdims=True)
    e2x = jnp.exp(m2 - m1)
    w1 = p_sel / (1.0 + e2x)
    w2 = p_sel * e2x / (1.0 + e2x)

    oh1 = col == (i1 - N_GROUPS)
    oh2 = col == (i2 - N_GROUPS)
    oh = jnp.where(oh1 | oh2, 1.0, 0.0)
    r_i = lax.broadcasted_iota(I32, (tm, tm), 0)
    c_i = lax.broadcasted_iota(I32, (tm, tm), 1)
    earlier = jnp.where(r_i > c_i, 1.0, 0.0).astype(BF16)
    before = jnp.dot(earlier, oh.astype(BF16), preferred_element_type=F32)
    cnt = jnp.sum(oh, axis=0, keepdims=True)
    k16 = jnp.floor((cnt + (BAND - 1.0)) * (1.0 / BAND))
    k16_8 = jnp.broadcast_to(k16, (SUBLANES, LANES))
    e_r = lax.broadcasted_iota(I32, (LANES, LANES), 0)
    e_c = lax.broadcasted_iota(I32, (LANES, LANES), 1)
    lower_e = jnp.where(e_r < e_c, 1.0, 0.0).astype(BF16)
    seg_start = jnp.dot(k16_8.astype(BF16), lower_e, preferred_element_type=F32)[0:1, :] * BAND
    base = before + seg_start
    lp1 = jnp.sum(jnp.where(oh1, base, 0.0), axis=1, keepdims=True)
    lp2 = jnp.sum(jnp.where(oh2, base, 0.0), axis=1, keepdims=True)

    info = jnp.where(col == 0, lp1, jnp.where(col == 1, lp2, jnp.where(col == 2, w1, jnp.where(col == 3, w2, 0.0))))
    info_ref[...] = info
    infot_ref[...] = info.T[0:SUBLANES, :]
    k16_ref[...] = k16_8


def _router_call(x, gains, mod, sc_blk, sh_blk, w_r, b_r, l):
    t, d = x.shape
    tm = MOE_TM
    nt = t // tm
    return pl.pallas_call(
        _router_kernel,
        grid=(nt,),
        in_specs=[pl.BlockSpec((tm, d), lambda i: (i, 0)),
                  pl.BlockSpec((None, 1, d), lambda i: (l, 0, 0)),
                  pl.BlockSpec((None, 1, d), lambda i: (l, 0, sc_blk)),
                  pl.BlockSpec((None, 1, d), lambda i: (l, 0, sh_blk)),
                  pl.BlockSpec((None, d, LANES), lambda i: (l, 0, 0)),
                  pl.BlockSpec((None, 1, LANES), lambda i: (l, 0, 0))],
        out_specs=[pl.BlockSpec((tm, d), lambda i: (i, 0)),
                   pl.BlockSpec((tm, LANES), lambda i: (i, 0)),
                   pl.BlockSpec((None, SUBLANES, tm), lambda i: (i, 0, 0)),
                   pl.BlockSpec((None, SUBLANES, LANES), lambda i: (i, 0, 0))],
        out_shape=[jax.ShapeDtypeStruct((t, d), BF16),
                   jax.ShapeDtypeStruct((t, LANES), F32),
                   jax.ShapeDtypeStruct((nt, SUBLANES, tm), F32),
                   jax.ShapeDtypeStruct((nt, SUBLANES, LANES), F32)],
        compiler_params=_cparams(("parallel",)),
        name="moe_router",
    )(x, gains, mod, mod, w_r, b_r)


def _moe_plan(k16, n_chunks):
    k = k16[:, 0, :N_EXPERTS].astype(I32)
    bpc = MOE_TMG // BAND
    tot = jnp.sum(k, axis=0)
    padc = ((tot + bpc - 1) // bpc) * bpc
    end_e = jnp.cumsum(padc)
    start_e = end_e - padc
    gstart = start_e[None, :] + jnp.cumsum(k, axis=0) - k
    lend = jnp.cumsum(k, axis=1)
    lstart = lend - k
    b = jnp.arange(MOE_MAXB, dtype=I32)
    eb = jnp.sum((b[None, :, None] >= lend[:, None, :]).astype(I32), axis=2)
    eb = jnp.minimum(eb, N_EXPERTS - 1)
    dstb = jnp.take_along_axis(gstart, eb, axis=1) + b[None, :] - jnp.take_along_axis(lstart, eb, axis=1)
    nbands = lend[:, -1]
    dstb = jnp.where(b[None, :] < nbands[:, None], dstb, 0)
    cstart = jnp.arange(n_chunks, dtype=I32) * bpc
    cexp = jnp.sum((cstart[:, None] >= end_e[None, :]).astype(I32), axis=1)
    cexp = jnp.minimum(cexp, N_EXPERTS - 1)
    n_used = (end_e[-1] // bpc).reshape(1)
    n_tiles = k.shape[0]
    padb = _pad_bands_per_tile(n_tiles)
    total_bands = n_chunks * bpc
    gap_start = jnp.concatenate([start_e + tot, end_e[-1:]])
    gap_len = jnp.concatenate([padc - tot, total_bands - end_e[-1:]])
    pend = jnp.cumsum(gap_len)
    pstart = pend - gap_len
    j = jnp.arange(n_tiles * padb, dtype=I32)
    gj = jnp.minimum(jnp.sum((j[:, None] >= pend[None, :]).astype(I32), axis=1), N_EXPERTS)
    dstp = jnp.take(gap_start, gj) + j - jnp.take(pstart, gj)
    dstp = jnp.where(j < pend[-1], dstp, 0)
    npad = jnp.clip(pend[-1] - jnp.arange(n_tiles, dtype=I32) * padb, 0, padb)
    return dstb.reshape(-1), nbands, dstp, npad, cexp, n_used


def _pad_bands_per_tile(n_tiles):
    worst = (n_tiles * MOE_MAXR + N_EXPERTS * MOE_TMG - 2 * n_tiles * MOE_TM) // BAND
    return -(-worst // n_tiles)


def _band_copy_out(xloc, xs_ref, dstb_ref, sem, tile, b):
    slot = tile % 2
    src = xloc.at[slot, pl.ds(pl.multiple_of(b * BAND, BAND), BAND), :]
    row = pl.multiple_of(dstb_ref[tile * MOE_MAXB + b] * BAND, BAND)
    return pltpu.make_async_copy(src, xs_ref.at[pl.ds(row, BAND), :], sem.at[slot])


def _zero_band_copy(xloc, xs_ref, dstp_ref, sem, padb, tile, j):
    slot = tile % 2
    row = pl.multiple_of(dstp_ref[tile * padb + j] * BAND, BAND)
    return pltpu.make_async_copy(xloc.at[slot, pl.ds(MOE_MAXR, BAND), :],
                                 xs_ref.at[pl.ds(row, BAND), :], sem.at[slot])


def _dispatch_kernel(dstb_ref, nb_ref, dstp_ref, npad_ref, h_ref, lpt_ref, xs_ref, xloc, sem, *, padb):
    tile = pl.program_id(0)
    last = pl.num_programs(0) - 1
    tm, d = h_ref.shape

    def all_copies(t, op):
        def data(b, carry):
            op(_band_copy_out(xloc, xs_ref, dstb_ref, sem, t, b))
            return carry

        def zero(j, carry):
            op(_zero_band_copy(xloc, xs_ref, dstp_ref, sem, padb, t, j))
            return carry

        lax.fori_loop(0, nb_ref[t], data, 0)
        lax.fori_loop(0, npad_ref[t], zero, 0)

    @pl.when(tile >= 2)
    def _():
        all_copies(tile - 2, lambda cp: cp.wait())

    slot = tile % 2
    lp1 = lpt_ref[0:1, :]
    lp2 = lpt_ref[1:2, :]
    rows = lax.broadcasted_iota(I32, (MOE_MAXR + BAND, tm), 0).astype(F32)
    perm = jnp.where((rows == lp1) | (rows == lp2), 1.0, 0.0).astype(BF16)
    for c0 in range(0, d, MOE_COLS):
        xloc[slot, :, c0:c0 + MOE_COLS] = jnp.dot(
            perm, h_ref[:, c0:c0 + MOE_COLS], preferred_element_type=F32).astype(BF16)
    all_copies(tile, lambda cp: cp.start())

    @pl.when(tile == last)
    def _():
        @pl.when(tile >= 1)
        def _():
            all_copies(tile - 1, lambda cp: cp.wait())
        all_copies(tile, lambda cp: cp.wait())


def _dispatch_call(h, infot, dstb, nbands, dstp, npad, n_rows):
    t, d = h.shape
    nt = t // MOE_TM
    return pl.pallas_call(
        functools.partial(_dispatch_kernel, padb=_pad_bands_per_tile(nt)),
        grid_spec=pltpu.PrefetchScalarGridSpec(
            num_scalar_prefetch=4,
            grid=(nt,),
            in_specs=[pl.BlockSpec((MOE_TM, d), lambda i, *_: (i, 0)),
                      pl.BlockSpec((None, SUBLANES, MOE_TM), lambda i, *_: (i, 0, 0))],
            out_specs=pl.BlockSpec(memory_space=pl.ANY),
            scratch_shapes=[pltpu.VMEM((2, MOE_MAXR + BAND, d), BF16), pltpu.SemaphoreType.DMA((2,))]),
        out_shape=jax.ShapeDtypeStruct((n_rows, d), BF16),
        compiler_params=_cparams(("arbitrary",)),
        name="moe_dispatch",
    )(dstb, nbands, dstp, npad, h, infot)


def _experts_kernel(cexp_ref, nused_ref, x_ref, wg_ref, wu_ref, wd_ref, y_ref, wgu_ref):
    f = wd_ref.shape[0]
    c = pl.program_id(0)
    prev = cexp_ref[jnp.maximum(c - 1, 0)]

    @pl.when((c == 0) | (cexp_ref[c] != prev))
    def _():
        wgu_ref[:, 0:f] = wg_ref[...]
        wgu_ref[:, f:] = wu_ref[...]

    @pl.when(c < nused_ref[0])
    def _():
        half = x_ref.shape[0] // 2
        parts = [slice(0, half), slice(half, 2 * half)]
        gus = [jnp.dot(x_ref[rows, :], wgu_ref[...], preferred_element_type=F32) for rows in parts]
        for rows, gu in zip(parts, gus):
            g, u = gu[:, 0:f], gu[:, f:]
            act = ((g * jax.nn.sigmoid(g)) * u).astype(BF16)
            y_ref[rows, :] = jnp.dot(act, wd_ref[...], preferred_element_type=F32).astype(y_ref.dtype)

    @pl.when(c >= nused_ref[0])
    def _():
        y_ref[...] = jnp.zeros_like(y_ref)


def _experts_call(xs, wg, wu, wd, l, cexp, n_used):
    n_rows, d = xs.shape
    f = wd.shape[2]
    n_chunks = n_rows // MOE_TMG
    return pl.pallas_call(
        _experts_kernel,
        grid_spec=pltpu.PrefetchScalarGridSpec(
            num_scalar_prefetch=2,
            grid=(n_chunks,),
            in_specs=[pl.BlockSpec((MOE_TMG, d), lambda c, ce, nu: (jnp.minimum(c, nu[0] - 1), 0)),
                      pl.BlockSpec((None, None, d, f), lambda c, ce, nu: (l, ce[c], 0, 0)),
                      pl.BlockSpec((None, None, d, f), lambda c, ce, nu: (l, ce[c], 0, 0)),
                      pl.BlockSpec((None, None, f, d), lambda c, ce, nu: (l, ce[c], 0, 0))],
            out_specs=pl.BlockSpec((MOE_TMG, d), lambda c, ce, nu: (c, 0)),
            scratch_shapes=[pltpu.VMEM((d, 2 * f), BF16)]),
        out_shape=jax.ShapeDtypeStruct((n_rows, d), BF16),
        compiler_params=_cparams(("arbitrary",)),
        name="moe_experts",
    )(cexp, n_used, xs, wg, wu, wd)


def _band_copy_in(y_ref, yloc, dstb_ref, sem, tile, b):
    slot = tile % 2
    row = pl.multiple_of(dstb_ref[tile * MOE_MAXB + b] * BAND, BAND)
    dst = yloc.at[slot, pl.ds(pl.multiple_of(b * BAND, BAND), BAND), :]
    return pltpu.make_async_copy(y_ref.at[pl.ds(row, BAND), :], dst, sem.at[slot])


def _combine_kernel(dstb_ref, nb_ref, y_ref, info_ref, x_ref, g_ref, o_ref, yloc, sem):
    tile = pl.program_id(0)
    last = pl.num_programs(0) - 1
    tm, d = x_ref.shape

    def fetch(t):
        def start(b, carry):
            _band_copy_in(y_ref, yloc, dstb_ref, sem, t, b).start()
            return carry

        def clear(b, carry):
            yloc[t % 2, pl.ds(pl.multiple_of(b * BAND, BAND), BAND), :] = jnp.zeros((BAND, d), yloc.dtype)
            return carry

        lax.fori_loop(0, nb_ref[t], start, 0)
        lax.fori_loop(nb_ref[t], MOE_MAXB, clear, 0)

    @pl.when(tile == 0)
    def _():
        fetch(tile)

    @pl.when(tile < last)
    def _():
        fetch(tile + 1)

    info = info_ref[...]
    lp1, lp2, w1, w2 = info[:, 0:1], info[:, 1:2], info[:, 2:3], info[:, 3:4]
    cols = lax.broadcasted_iota(I32, (tm, MOE_MAXR), 1).astype(F32)
    pw = (jnp.where(cols == lp1, w1, 0.0) + jnp.where(cols == lp2, w2, 0.0)).astype(BF16)

    def wait(b, carry):
        _band_copy_in(y_ref, yloc, dstb_ref, sem, tile, b).wait()
        return carry

    lax.fori_loop(0, nb_ref[tile], wait, 0)
    slot = tile % 2
    for c0 in range(0, d, MOE_COLS):
        sl = slice(c0, c0 + MOE_COLS)
        acc = jnp.dot(pw, yloc[slot, :, sl], preferred_element_type=F32)
        o_ref[:, sl] = x_ref[:, sl] + g_ref[:, sl] * acc


def _combine_call(y, info, x, mod, l, gate_blk, dstb, nbands):
    t, d = x.shape
    nt = t // MOE_TM
    return pl.pallas_call(
        _combine_kernel,
        grid_spec=pltpu.PrefetchScalarGridSpec(
            num_scalar_prefetch=2,
            grid=(nt,),
            in_specs=[pl.BlockSpec(memory_space=pl.ANY),
                      pl.BlockSpec((MOE_TM, LANES), lambda i, *_: (i, 0)),
                      pl.BlockSpec((MOE_TM, d), lambda i, *_: (i, 0)),
                      pl.BlockSpec((None, 1, d), lambda i, *_: (l, 0, gate_blk))],
            out_specs=pl.BlockSpec((MOE_TM, d), lambda i, *_: (i, 0)),
            scratch_shapes=[pltpu.VMEM((2, MOE_MAXR, d), BF16), pltpu.SemaphoreType.DMA((2,))]),
        out_shape=jax.ShapeDtypeStruct((t, d), F32),
        compiler_params=_cparams(("arbitrary",)),
        name="moe_combine",
    )(dstb, nbands, y, info, x, mod)


def kernel(x, c, w_mod, mod_table, norm1_g, w_in, ssm_a_re, ssm_a_im, ssm_log_dt, ssm_b_re, ssm_b_im, ssm_c_re, ssm_c_im, ssm_d, w_glu, w_br_ssm, conv_w, w_br_conv, w_o, norm2_g, w_router_group, b_router_group, w_router_expert, b_router_expert, w_exp_gate, w_exp_up, w_exp_down, final_g):
    bsz, seq, d = x.shape
    depth = mod_table.shape[0]
    h_ssm = ssm_d.shape[1]
    h_conv = conv_w.shape[2]
    n_exp, _, d_expert = w_exp_gate.shape[1:]
    assert bsz == 1 and seq % SSM_TT == 0 and h_ssm % SSM_CB == 0 and seq % MOE_TM == 0
    assert ssm_a_re.shape[1:] == (h_ssm // SSM_GROUP, SSM_STATE) and n_exp == N_EXPERTS
    assert d % MOE_COLS == 0
    u_col, b_col, c_col, v_col = 0, h_ssm, h_ssm + h_conv, h_ssm + 2 * h_conv
    gs_col = h_ssm + 3 * h_conv
    gc_col = gs_col + d
    sh1, sc1, g1, sh2, sc2, g2 = range(N_MOD)

    w_in_b = w_in.astype(BF16)
    w_glu_b = w_glu.astype(BF16)
    w_brs_b = w_br_ssm.astype(BF16)
    w_brc_b = w_br_conv.astype(BF16)
    w_o_b = w_o.astype(BF16)
    wg_b = w_exp_gate.astype(BF16)
    wu_b = w_exp_up.astype(BF16)
    wd_b = w_exp_down.astype(BF16)
    pad = LANES - N_GROUPS - N_EXPERTS
    w_r = jnp.concatenate([w_router_group, w_router_expert, jnp.zeros((depth, d, pad), F32)], axis=2).astype(BF16)
    b_r = jnp.concatenate([b_router_group, b_router_expert, jnp.zeros((depth, pad), F32)], axis=1)[:, None, :]
    s5_kc, s5_pc, s5_qc, ltab = _ssm_tables(ssm_a_re, ssm_a_im, ssm_log_dt, ssm_b_re, ssm_b_im,
                                           ssm_c_re, ssm_c_im)

    n_tiles = seq // MOE_TM
    n_rows = n_tiles * MOE_MAXR + N_EXPERTS * MOE_TMG
    n_chunks = n_rows // MOE_TMG

    xt = x.reshape(seq, d)
    mod = _mod_call(c, w_mod, mod_table)[:, None, :]
    norm1_g, norm2_g, ssm_d = norm1_g[:, None, :], norm2_g[:, None, :], ssm_d[:, None, :]

    for l in range(depth):
        h1 = _norm_mod_call(xt, norm1_g, mod, l, sc1, sh1)
        proj = _mm_call(h1, w_in_b, l, F32, 1024, 512, "in_proj")
        y_pre = _ssm_call(proj, u_col, h_ssm, s5_kc, s5_pc, s5_qc, ltab, ssm_d, l)
        ys = _glu_call(y_pre, w_glu_b, l)
        yc = _conv_call(proj, conv_w, l, b_col, c_col, v_col, h_conv)
        merged = _merge_call(ys, yc, w_brs_b, w_brc_b, l, proj, gs_col, gc_col)
        xt = _resid_call(merged, w_o_b, l, xt, mod, g1, 1024, 512, "out_proj")

        h2, info, infot, k16 = _router_call(xt, norm2_g, mod, sc2, sh2, w_r, b_r, l)
        dstb, nbands, dstp, npad, cexp, n_used = _moe_plan(k16, n_chunks)
        xs = _dispatch_call(h2, infot, dstb, nbands, dstp, npad, n_rows)
        ye = _experts_call(xs, wg_b, wu_b, wd_b, l, cexp, n_used)
        xt = _combine_call(ye, info, xt, mod, l, g2, dstb, nbands)

    out = _norm_call(xt, final_g[None, :])
    return out.reshape(bsz, seq, d)
```

```python
import functools
import math

import jax
import jax.numpy as jnp
from jax import lax
from jax.experimental import pallas as pl
from jax.experimental.pallas import tpu as pltpu

F32 = jnp.float32
BF16 = jnp.bfloat16
I32 = jnp.int32

EPS = 1e-6
SSM_GROUP = 16
SSM_STATE = 64
CONV_WIDTH = 3
N_GROUPS = 4
EXPERTS_PER_GROUP = 4
N_EXPERTS = N_GROUPS * EXPERTS_PER_GROUP
N_MOD = 6

LANES = 128
SUBLANES = 8
VMEM_LIMIT_BYTES = 58 * 1024 * 1024

SSM_LC = 16
SSM_LA = 4
SSM_LB = 4
SSM_S2 = 16
SSM_K2 = 16
SSM_NK = SSM_S2 * SSM_K2
SSM_TT = SSM_NK * SSM_LC
SSM_PITCH = SSM_S2 * SSM_LC + SUBLANES
SSM_CB = 256
SSM_GB = SSM_CB // SSM_GROUP
SSM_NS = SSM_GB * SSM_STATE
LT_ROWS = 24
SSM_NPOW = 32

MOE_TM = 512
BAND = 2 * SUBLANES
MOE_MAXR = 2 * MOE_TM + N_EXPERTS * BAND
MOE_MAXB = MOE_MAXR // BAND
MOE_TMG = 512
MOE_COLS = 1024


def _cparams(sem):
    return pltpu.CompilerParams(dimension_semantics=sem, vmem_limit_bytes=VMEM_LIMIT_BYTES)


def _mod_kernel(c_ref, w_ref, t_ref, o_ref):
    c = c_ref[...]
    s = c * jax.nn.sigmoid(c)
    shared = jnp.sum(s * w_ref[...], axis=0, keepdims=True)
    o_ref[...] = t_ref[...] + shared


def _mod_call(c, w_mod, mod_table):
    d, n = w_mod.shape
    depth = mod_table.shape[0]
    tn = 512
    return pl.pallas_call(
        _mod_kernel,
        grid=(n // tn,),
        in_specs=[pl.BlockSpec((d, 1), lambda j: (0, 0)),
                  pl.BlockSpec((d, tn), lambda j: (0, j)),
                  pl.BlockSpec((depth, tn), lambda j: (0, j))],
        out_specs=pl.BlockSpec((depth, tn), lambda j: (0, j)),
        out_shape=jax.ShapeDtypeStruct((depth, n), F32),
        compiler_params=_cparams(("arbitrary",)),
        name="adaln_mod",
    )(c.reshape(d, 1), w_mod, mod_table)


def _norm_mod_kernel(x_ref, g_ref, sc_ref, sh_ref, o_ref):
    x = x_ref[...]
    y = x * lax.rsqrt(jnp.mean(x * x, axis=-1, keepdims=True) + EPS)
    o_ref[...] = ((y * g_ref[...]) * (1.0 + sc_ref[...]) + sh_ref[...]).astype(o_ref.dtype)


def _norm_kernel(x_ref, g_ref, o_ref):
    x = x_ref[...]
    y = x * lax.rsqrt(jnp.mean(x * x, axis=-1, keepdims=True) + EPS)
    o_ref[...] = (y * g_ref[...]).astype(o_ref.dtype)


def _norm_mod_call(x, gains, mod, l, sc_blk, sh_blk, tm=256):
    t, d = x.shape
    return pl.pallas_call(
        _norm_mod_kernel,
        grid=(t // tm,),
        in_specs=[pl.BlockSpec((tm, d), lambda i: (i, 0)),
                  pl.BlockSpec((None, 1, d), lambda i: (l, 0, 0)),
                  pl.BlockSpec((None, 1, d), lambda i: (l, 0, sc_blk)),
                  pl.BlockSpec((None, 1, d), lambda i: (l, 0, sh_blk))],
        out_specs=pl.BlockSpec((tm, d), lambda i: (i, 0)),
        out_shape=jax.ShapeDtypeStruct((t, d), BF16),
        compiler_params=_cparams(("parallel",)),
        name="norm_mod",
    )(x, gains, mod, mod)


def _norm_call(x, g, tm=256):
    t, d = x.shape
    return pl.pallas_call(
        _norm_kernel,
        grid=(t // tm,),
        in_specs=[pl.BlockSpec((tm, d), lambda i: (i, 0)), pl.BlockSpec((1, d), lambda i: (0, 0))],
        out_specs=pl.BlockSpec((tm, d), lambda i: (i, 0)),
        out_shape=jax.ShapeDtypeStruct((t, d), F32),
        compiler_params=_cparams(("parallel",)),
        name="final_norm",
    )(x, g)


def _wspec(l, k, tn):
    return pl.BlockSpec((None, k, tn), lambda i, j: (l, 0, j))


def _mm_kernel(a_ref, w_ref, o_ref, *, gate):
    acc = jnp.dot(a_ref[...], w_ref[...], preferred_element_type=F32)
    o_ref[...] = (jax.nn.sigmoid(acc) if gate else acc).astype(o_ref.dtype)


def _mm_call(a, w, l, col0, n, gate, out_dtype, tm, tn, name):
    t, k = a.shape
    cb = col0 // tn
    return pl.pallas_call(
        functools.partial(_mm_kernel, gate=gate),
        grid=(t // tm, n // tn),
        in_specs=[pl.BlockSpec((tm, k), lambda i, j: (i, 0)),
                  pl.BlockSpec((None, k, tn), lambda i, j: (l, 0, cb + j))],
        out_specs=pl.BlockSpec((tm, tn), lambda i, j: (i, j)),
        out_shape=jax.ShapeDtypeStruct((t, n), out_dtype),
        compiler_params=_cparams(("parallel", "arbitrary")),
        name=name,
    )(a, w)


def _glu_kernel(a_ref, y_ref, w_ref, o_ref):
    acc = jnp.dot(a_ref[...], w_ref[...], preferred_element_type=F32)
    o_ref[...] = (y_ref[...].astype(F32) * jax.nn.sigmoid(acc)).astype(o_ref.dtype)


def _glu_call(y, w, l, tm=1024, tn=512):
    t, k = y.shape
    n = w.shape[2]
    return pl.pallas_call(
        _glu_kernel,
        grid=(t // tm, n // tn),
        in_specs=[pl.BlockSpec((tm, k), lambda i, j: (i, 0)),
                  pl.BlockSpec((tm, tn), lambda i, j: (i, j)),
                  _wspec(l, k, tn)],
        out_specs=pl.BlockSpec((tm, tn), lambda i, j: (i, j)),
        out_shape=jax.ShapeDtypeStruct((t, n), BF16),
        compiler_params=_cparams(("parallel", "arbitrary")),
        name="s5_glu",
    )(y, y, w)


def _merge_kernel(ys_ref, yc_ref, ws_ref, wc_ref, gs_ref, gc_ref, o_ref):
    y_ssm = jnp.dot(ys_ref[...], ws_ref[...], preferred_element_type=F32)
    y_conv = jnp.dot(yc_ref[...], wc_ref[...], preferred_element_type=F32)
    merged = gs_ref[...].astype(F32) * y_ssm + gc_ref[...].astype(F32) * y_conv
    o_ref[...] = merged.astype(o_ref.dtype)


def _merge_call(ys, yc, ws, wc, l, gates, tm=1024, tn=512):
    t, k = ys.shape
    n = ws.shape[2]
    gs_blk, gc_blk = 0, n // tn
    return pl.pallas_call(
        _merge_kernel,
        grid=(t // tm, n // tn),
        in_specs=[pl.BlockSpec((tm, k), lambda i, j: (i, 0)),
                  pl.BlockSpec((tm, k), lambda i, j: (i, 0)),
                  _wspec(l, k, tn),
                  _wspec(l, k, tn),
                  pl.BlockSpec((tm, tn), lambda i, j: (i, gs_blk + j)),
                  pl.BlockSpec((tm, tn), lambda i, j: (i, gc_blk + j))],
        out_specs=pl.BlockSpec((tm, tn), lambda i, j: (i, j)),
        out_shape=jax.ShapeDtypeStruct((t, n), BF16),
        compiler_params=_cparams(("parallel", "arbitrary")),
        name="branch_merge",
    )(ys, yc, ws, wc, gates, gates)


def _resid_kernel(a_ref, w_ref, x_ref, g_ref, o_ref):
    acc = jnp.dot(a_ref[...], w_ref[...], preferred_element_type=F32)
    o_ref[...] = x_ref[...] + g_ref[...] * acc


def _resid_call(a, w, l, x, mod, gate_blk, tm, tn, name):
    t, k = a.shape
    n = w.shape[2]
    gb = gate_blk * (n // tn)
    return pl.pallas_call(
        _resid_kernel,
        grid=(t // tm, n // tn),
        in_specs=[pl.BlockSpec((tm, k), lambda i, j: (i, 0)),
                  _wspec(l, k, tn),
                  pl.BlockSpec((tm, tn), lambda i, j: (i, j)),
                  pl.BlockSpec((None, 1, tn), lambda i, j: (l, 0, gb + j))],
        out_specs=pl.BlockSpec((tm, tn), lambda i, j: (i, j)),
        out_shape=jax.ShapeDtypeStruct((t, n), F32),
        compiler_params=_cparams(("parallel", "arbitrary")),
        name=name,
    )(a, w, x, mod)


def _proj_conv_kernel(a_ref, wb_ref, wc_ref, wv_ref, w_ref, o_ref, z_scr, carry):
    tm = a_ref.shape[0]
    halo = SUBLANES
    i, k = pl.program_id(0), pl.program_id(1)

    @pl.when((i == 0) & (k == 0))
    def _():
        carry[...] = jnp.zeros_like(carry)

    a = a_ref[...]
    bg = jnp.dot(a, wb_ref[...], preferred_element_type=F32)
    z = (jnp.dot(a, wc_ref[...], preferred_element_type=F32)
         * jnp.dot(a, wv_ref[...], preferred_element_type=F32))
    z_scr[0:halo, :] = carry[k]
    z_scr[halo:, :] = z
    carry[k] = z[tm - halo:, :]
    w = w_ref[...]
    conv = w[2:3, :] * z
    for tap in range(CONV_WIDTH - 1):
        conv = conv + w[tap:tap + 1, :] * z_scr[pl.ds(halo - (CONV_WIDTH - 1 - tap), tm), :]
    o_ref[...] = (bg * conv).astype(o_ref.dtype)


def _proj_conv_call(a, w, l, conv_w, b_col, c_col, v_col, width, tm=1024, tc=256):
    t, kdim = a.shape
    bb, cb, vb = b_col // tc, c_col // tc, v_col // tc

    def wspec(off):
        return pl.BlockSpec((None, kdim, tc), lambda i, k: (l, 0, off + k))

    return pl.pallas_call(
        _proj_conv_kernel,
        grid=(t // tm, width // tc),
        in_specs=[pl.BlockSpec((tm, kdim), lambda i, k: (i, 0)),
                  wspec(bb), wspec(cb), wspec(vb),
                  pl.BlockSpec((None, CONV_WIDTH, tc), lambda i, k: (l, 0, k))],
        out_specs=pl.BlockSpec((tm, tc), lambda i, k: (i, k)),
        out_shape=jax.ShapeDtypeStruct((t, width), BF16),
        scratch_shapes=[pltpu.VMEM((tm + SUBLANES, tc), F32),
                        pltpu.VMEM((width // tc, SUBLANES, tc), F32)],
        compiler_params=_cparams(("arbitrary", "arbitrary")),
        name="in_proj_conv",
    )(a, w, w, w, conv_w)


def _cmul_row(xr, xi, lr, li):
    return xr * lr - xi * li, xr * li + xi * lr


def _gelu_tanh(x):
    return 0.5 * x * (1.0 + jnp.tanh(math.sqrt(2.0 / math.pi) * (x + 0.044715 * (x * x * x))))


def _expand_block_diag(dst, spread_rows, row_shift, col_shift, chunk):
    n_r, n_c = dst.shape
    gmask = SSM_GB - 1
    for r0 in range(0, n_r, chunk):
        spread = spread_rows(r0, chunk)
        rg = (lax.broadcasted_iota(I32, (chunk, n_c), 0) + r0) >> row_shift
        cg = lax.broadcasted_iota(I32, (chunk, n_c), 1) >> col_shift
        keep = (rg & gmask) == (cg & gmask)
        dst[r0:r0 + chunk, :] = jnp.where(keep, spread, 0.0).astype(dst.dtype)


def _ssm_build_tables(pc_ref, qc_ref, kc_ref, p_scr, q_scr, k_scr):
    ns2 = 2 * SSM_NS
    p_state = SSM_STATE
    lp, lc = int(math.log2(p_state)), int(math.log2(SSM_GROUP))
    lns = int(math.log2(SSM_NS))
    r = lax.broadcasted_iota(I32, (LANES, ns2), 0)
    c = lax.broadcasted_iota(I32, (LANES, ns2), 1)
    tile_p = jnp.where(((r >> lp) == (c >> lns)) & ((r & (p_state - 1)) == (c & (p_state - 1))),
                       1.0, 0.0).astype(BF16)

    def p_rows(r0, chunk):
        return jnp.dot(pc_ref[r0:r0 + chunk, :], tile_p, preferred_element_type=F32)

    _expand_block_diag(p_scr, p_rows, lc, lp, 256)
    r = lax.broadcasted_iota(I32, (ns2, LANES), 0)
    c = lax.broadcasted_iota(I32, (ns2, LANES), 1)
    tile_q = jnp.where(((c >> lp) == (r >> lns)) & ((c & (p_state - 1)) == (r & (p_state - 1))),
                       1.0, 0.0).astype(BF16)

    def q_rows(r0, chunk):
        return jnp.dot(tile_q[r0:r0 + chunk, :], qc_ref[...], preferred_element_type=F32)

    _expand_block_diag(q_scr, q_rows, lp, lc, 512)
    r = lax.broadcasted_iota(I32, (LANES, SSM_CB), 0)
    c = lax.broadcasted_iota(I32, (LANES, SSM_CB), 1)
    tile_k = jnp.where(r == (c & (SSM_GROUP - 1)), 1.0, 0.0).astype(BF16)

    def k_rows(r0, chunk):
        return jnp.dot(kc_ref[r0:r0 + chunk, :], tile_k, preferred_element_type=F32)

    _expand_block_diag(k_scr, k_rows, lc, lc, 1024)


def _ssm_kernel(u0_ref, u1_ref, kc_ref, pc_ref, qc_ref, lt_ref, d_ref, o_ref,
                ub2, y2, bx, s_scr, ha, cs, carry, yo0, yo1, kall_ref, p_ref, q_ref, up0, up1):
    ns = SSM_NS
    cb = SSM_CB
    rows = SSM_K2
    ksteps = SSM_S2 * SSM_LC
    kstride = SSM_PITCH
    u_halves = (up0, up1)
    yo = (yo0, yo1)

    for src, dst in ((u0_ref, up0), (u1_ref, up1)):
        for k2 in range(SSM_K2):
            dst[k2 * kstride:k2 * kstride + ksteps, :] = src[k2 * ksteps:(k2 + 1) * ksteps, :]

    @pl.when(pl.program_id(1) == 0)
    def _():
        carry[...] = jnp.zeros_like(carry)
        _ssm_build_tables(pc_ref, qc_ref, kc_ref, p_ref, q_ref, kall_ref)

    def lt(row):
        return lt_ref[row:row + 1, 0:ns], lt_ref[row:row + 1, ns:2 * ns]

    for s in range(SSM_LC):
        for s2 in range(SSM_S2):
            idx = pl.ds(s2 * SSM_LC + s, rows, stride=kstride)
            for hf, u_ref in enumerate(u_halves):
                c0 = s * cb + hf * LANES
                ub2[s2 * rows:(s2 + 1) * rows, c0:c0 + LANES] = u_ref[idx, :].astype(BF16)

    for sp in range(SSM_LC):
        y2[:, sp * cb:(sp + 1) * cb] = jnp.dot(
            ub2[:, 0:(sp + 1) * cb], kall_ref[(SSM_LC - 1 - sp) * cb:, :],
            preferred_element_type=F32)

    s_scr[...] = jnp.dot(ub2[:, (SSM_LA - 1) * SSM_LB * cb:], p_ref[...], preferred_element_type=F32)
    for a in range(SSM_LA - 1):
        bx[...] = jnp.dot(ub2[:, a * SSM_LB * cb:(a + 1) * SSM_LB * cb], p_ref[...],
                          preferred_element_type=F32)
        lr, li = lt(SSM_LA - 2 - a)
        for s2 in range(SSM_S2):
            sl = slice(s2 * rows, (s2 + 1) * rows)
            pr, pi = _cmul_row(bx[sl, 0:ns], bx[sl, ns:], lr, li)
            s_scr[sl, 0:ns] += pr
            s_scr[sl, ns:] += pi

    lr, li = lt(4 + 1)
    for s2 in range(1, SSM_S2):
        pv = slice((s2 - 1) * rows, s2 * rows)
        sl = slice(s2 * rows, (s2 + 1) * rows)
        pr, pi = _cmul_row(s_scr[pv, 0:ns], s_scr[pv, ns:], lr, li)
        s_scr[sl, 0:ns] += pr
        s_scr[sl, ns:] += pi

    lr, li = lt(3)
    last = slice((SSM_S2 - 1) * rows, SSM_S2 * rows)
    zr, zi = s_scr[last, 0:ns], s_scr[last, ns:]
    cr, ci = carry[0:1, 0:ns], carry[0:1, ns:]
    for k2 in range(SSM_K2):
        cs[k2:k2 + 1, 0:ns] = cr
        cs[k2:k2 + 1, ns:] = ci
        pr, pi = _cmul_row(cr, ci, lr, li)
        cr, ci = pr + zr[k2:k2 + 1, :], pi + zi[k2:k2 + 1, :]
    carry[0:1, 0:ns] = cr
    carry[0:1, ns:] = ci

    for a in range(SSM_LA):
        for s2 in range(SSM_S2):
            sl = slice(s2 * rows, (s2 + 1) * rows)
            hr, hi = cs[:, 0:ns], cs[:, ns:]
            if s2 > 0:
                pv = slice((s2 - 1) * rows, s2 * rows)
                lr, li = lt(4 + s2)
                pr, pi = _cmul_row(hr, hi, lr, li)
                hr, hi = s_scr[pv, 0:ns] + pr, s_scr[pv, ns:] + pi
            if a > 0:
                lr, li = lt(a - 1)
                hr, hi = _cmul_row(hr, hi, lr, li)
            ha[sl, 0:ns] = hr.astype(BF16)
            ha[sl, ns:] = hi.astype(BF16)
        wa = SSM_LB * cb
        y2[:, a * wa:(a + 1) * wa] += jnp.dot(ha[...], q_ref[...], preferred_element_type=F32)

    d = d_ref[...]
    for s in range(SSM_LC):
        for s2 in range(SSM_S2):
            idx = pl.ds(s2 * SSM_LC + s, rows, stride=kstride)
            for hf, u_ref in enumerate(u_halves):
                c0 = s * cb + hf * LANES
                y = (y2[s2 * rows:(s2 + 1) * rows, c0:c0 + LANES]
                     + d[:, hf * LANES:(hf + 1) * LANES] * u_ref[idx, :])
                yo[hf][idx, :] = _gelu_tanh(y)
    for hf in range(len(u_halves)):
        for k2 in range(SSM_K2):
            o_ref[k2 * ksteps:(k2 + 1) * ksteps, hf * LANES:(hf + 1) * LANES] = (
                yo[hf][k2 * kstride:k2 * kstride + ksteps, :].astype(o_ref.dtype))


def _ssm_call(proj, u_col, h_ssm, kc, pc, qc, ltab, d_skip, l):
    t = proj.shape[0]
    nb = h_ssm // SSM_CB
    ub = u_col // SSM_CB
    ns2 = 2 * SSM_NS

    def tab(r, c):
        return pl.BlockSpec((None, None, r, c), lambda b, i: (l, b, 0, 0))

    return pl.pallas_call(
        _ssm_kernel,
        grid=(nb, t // SSM_TT),
        in_specs=[pl.BlockSpec((SSM_TT, LANES), lambda b, i: (i, 2 * (ub + b))),
                  pl.BlockSpec((SSM_TT, LANES), lambda b, i: (i, 2 * (ub + b) + 1)),
                  tab(SSM_LC * SSM_CB, LANES),
                  tab(SSM_LB * SSM_CB, LANES),
                  tab(LANES, SSM_LB * SSM_CB),
                  tab(LT_ROWS, ns2),
                  pl.BlockSpec((None, 1, SSM_CB), lambda b, i: (l, 0, b))],
        out_specs=pl.BlockSpec((SSM_TT, SSM_CB), lambda b, i: (i, b)),
        out_shape=jax.ShapeDtypeStruct((t, h_ssm), BF16),
        scratch_shapes=[pltpu.VMEM((SSM_NK, SSM_LC * SSM_CB), BF16),
                        pltpu.VMEM((SSM_NK, SSM_LC * SSM_CB), F32),
                        pltpu.VMEM((SSM_NK, ns2), F32),
                        pltpu.VMEM((SSM_NK, ns2), F32),
                        pltpu.VMEM((SSM_NK, ns2), BF16),
                        pltpu.VMEM((SSM_K2, ns2), F32),
                        pltpu.VMEM((SUBLANES, ns2), F32),
                        pltpu.VMEM((SSM_K2 * SSM_PITCH, LANES), F32),
                        pltpu.VMEM((SSM_K2 * SSM_PITCH, LANES), F32),
                        pltpu.VMEM((SSM_LC * SSM_CB, SSM_CB), BF16),
                        pltpu.VMEM((SSM_LB * SSM_CB, ns2), BF16),
                        pltpu.VMEM((ns2, SSM_LB * SSM_CB), BF16),
                        pltpu.VMEM((SSM_K2 * SSM_PITCH, LANES), F32),
                        pltpu.VMEM((SSM_K2 * SSM_PITCH, LANES), F32)],
        compiler_params=_cparams(("arbitrary", "arbitrary")),
        name="s5_scan",
    )(proj, proj, kc, pc, qc, ltab, d_skip)


def _ssm_tables(a_re, a_im, log_dt, b_re, b_im, c_re, c_im):
    assert 2 * a_re.shape[2] == LANES
    nl, g, p = a_re.shape
    gc = SSM_GROUP
    gb = SSM_GB
    nb = g // gb
    hi = lax.Precision.HIGHEST
    delta = jnp.exp(log_dt)[..., None]
    mag = jnp.exp(delta * a_re)
    ang = delta * a_im
    lam = (mag * jnp.cos(ang), mag * jnp.sin(ang))
    den = a_re * a_re + a_im * a_im
    zr, zi = lam[0] - 1.0, lam[1]
    coef_re = (zr * a_re + zi * a_im) / den
    coef_im = (zi * a_re - zr * a_im) / den
    bt_re = coef_re[..., None] * b_re - coef_im[..., None] * b_im
    bt_im = coef_re[..., None] * b_im + coef_im[..., None] * b_re

    def cmul(x, y):
        return x[0] * y[0] - x[1] * y[1], x[0] * y[1] + x[1] * y[0]

    def power_table(base):
        tab = (jnp.stack([jnp.ones_like(base[0]), base[0]]), jnp.stack([jnp.zeros_like(base[1]), base[1]]))
        step = cmul(base, base)
        while tab[0].shape[0] < SSM_NPOW:
            nxt = cmul(tab, (step[0][None], step[1][None]))
            tab = (jnp.concatenate([tab[0], nxt[0]]), jnp.concatenate([tab[1], nxt[1]]))
            step = cmul(step, step)
        return tab

    pw = power_table(lam)
    pw16 = power_table((pw[0][SSM_LC], pw[1][SSM_LC]))

    wr = pw[0][:SSM_LC, ..., None] * bt_re[None] - pw[1][:SSM_LC, ..., None] * bt_im[None]
    wi = pw[0][:SSM_LC, ..., None] * bt_im[None] + pw[1][:SSM_LC, ..., None] * bt_re[None]

    k = (jnp.einsum("lgop,tlgpi->tlgio", c_re, wr, precision=hi)
         - jnp.einsum("lgop,tlgpi->tlgio", c_im, wi, precision=hi))
    k = k[::-1].reshape(SSM_LC, nl, nb, gb, gc, gc).transpose(1, 2, 0, 3, 4, 5)
    kc = k.reshape(nl, nb, SSM_LC * SSM_CB, gc)
    kc = jnp.pad(kc, ((0, 0), (0, 0), (0, 0), (0, LANES - gc))).astype(BF16)

    def p_rows(w):
        return w[SSM_LB - 1::-1].reshape(SSM_LB, nl, nb, gb, p, gc).transpose(1, 2, 0, 3, 5, 4)

    pc = jnp.stack([p_rows(wr), p_rows(wi)], axis=5)
    pc = pc.reshape(nl, nb, SSM_LB * SSM_CB, 2 * p).astype(BF16)

    qw = (pw[0][1:SSM_LB + 1, :, :, None, :], pw[1][1:SSM_LB + 1, :, :, None, :])
    dr = c_re[None] * qw[0] - c_im[None] * qw[1]
    di = c_re[None] * qw[1] + c_im[None] * qw[0]

    def q_cols(w):
        return w.reshape(SSM_LB, nl, nb, gb, gc, p).transpose(1, 2, 5, 0, 3, 4)

    qc = jnp.stack([q_cols(dr), q_cols(-di)], axis=2)
    qc = qc.reshape(nl, nb, 2 * p, SSM_LB * SSM_CB).astype(BF16)

    def lrows(part):
        r = jnp.concatenate([pw[part][SSM_LB:SSM_LC:SSM_LB], pw16[part][SSM_S2:SSM_S2 + 1],
                             pw16[part][:SSM_S2]])
        r = jnp.concatenate([r, jnp.zeros((LT_ROWS - r.shape[0],) + r.shape[1:], F32)])
        return r.reshape(LT_ROWS, nl, nb, SSM_NS).transpose(1, 2, 0, 3)

    ltab = jnp.concatenate([lrows(0), lrows(1)], axis=3)
    return kc, pc, qc, ltab


def _router_kernel(x_ref, gain_ref, sc_ref, sh_ref, w_ref, b_ref, h_ref, info_ref, infot_ref, k16_ref):
    tm = x_ref.shape[0]
    x = x_ref[...]
    y = x * lax.rsqrt(jnp.mean(x * x, axis=-1, keepdims=True) + EPS)
    h = ((y * gain_ref[...]) * (1.0 + sc_ref[...]) + sh_ref[...]).astype(h_ref.dtype)
    h_ref[...] = h
    lg = jnp.dot(h, w_ref[...], preferred_element_type=F32) + b_ref[...]
    col = lax.broadcasted_iota(I32, lg.shape, 1)
    big = jnp.int32(LANES)
    neg = -jnp.inf
    is_g = col < N_GROUPS
    lgg = jnp.where(is_g, lg, neg)
    gmax = jnp.max(lgg, axis=1, keepdims=True)
    gsel = jnp.min(jnp.where(is_g & (lgg == gmax), col, big), axis=1, keepdims=True)
    denom = jnp.sum(jnp.where(is_g, jnp.exp(lgg - gmax), 0.0), axis=1, keepdims=True)
    p_sel = 1.0 / denom
    ecol = col - N_GROUPS
    egrp = lax.shift_right_arithmetic(ecol, jnp.int32(int(math.log2(EXPERTS_PER_GROUP))))
    in_grp = (ecol >= 0) & (ecol < N_EXPERTS) & (egrp == gsel)
    le = jnp.where(in_grp, lg, neg)
    m1 = jnp.max(le, axis=1, keepdims=True)
    i1 = jnp.min(jnp.where(in_grp & (le == m1), col, big), axis=1, keepdims=True)
    rest = in_grp & (col != i1)
    le2 = jnp.where(rest, lg, neg)
    m2 = jnp.max(le2, axis=1, keepdims=True)
    i2 = jnp.min(jnp.where(rest & (le2 == m2), col, big), axis=1, keepdims=True)
    e2x = jnp.exp(m2 - m1)
    w1 = p_sel / (1.0 + e2x)
    w2 = p_sel * e2x / (1.0 + e2x)

    oh1 = col == (i1 - N_GROUPS)
    oh2 = col == (i2 - N_GROUPS)
    oh = jnp.where(oh1 | oh2, 1.0, 0.0)
    r_i = lax.broadcasted_iota(I32, (tm, tm), 0)
    c_i = lax.broadcasted_iota(I32, (tm, tm), 1)
    earlier = jnp.where(r_i > c_i, 1.0, 0.0).astype(BF16)
    before = jnp.dot(earlier, oh.astype(BF16), preferred_element_type=F32)
    cnt = jnp.sum(oh, axis=0, keepdims=True)
    k16 = jnp.floor((cnt + (BAND - 1.0)) * (1.0 / BAND))
    k16_8 = jnp.broadcast_to(k16, (SUBLANES, LANES))
    e_r = lax.broadcasted_iota(I32, (LANES, LANES), 0)
    e_c = lax.broadcasted_iota(I32, (LANES, LANES), 1)
    lower_e = jnp.where(e_r < e_c, 1.0, 0.0).astype(BF16)
    seg_start = jnp.dot(k16_8.astype(BF16), lower_e, preferred_element_type=F32)[0:1, :] * BAND
    base = before + seg_start
    lp1 = jnp.sum(jnp.where(oh1, base, 0.0), axis=1, keepdims=True)
    lp2 = jnp.sum(jnp.where(oh2, base, 0.0), axis=1, keepdims=True)

    info = jnp.where(col == 0, lp1, jnp.where(col == 1, lp2, jnp.where(col == 2, w1, jnp.where(col == 3, w2, 0.0))))
    info_ref[...] = info
    infot_ref[...] = info.T[0:SUBLANES, :]
    k16_ref[...] = k16_8


def _router_call(x, gains, mod, sc_blk, sh_blk, w_r, b_r, l):
    t, d = x.shape
    tm = MOE_TM
    nt = t // tm
    return pl.pallas_call(
        _router_kernel,
        grid=(nt,),
        in_specs=[pl.BlockSpec((tm, d), lambda i: (i, 0)),
                  pl.BlockSpec((None, 1, d), lambda i: (l, 0, 0)),
                  pl.BlockSpec((None, 1, d), lambda i: (l, 0, sc_blk)),
                  pl.BlockSpec((None, 1, d), lambda i: (l, 0, sh_blk)),
                  pl.BlockSpec((None, d, LANES), lambda i: (l, 0, 0)),
                  pl.BlockSpec((None, 1, LANES), lambda i: (l, 0, 0))],
        out_specs=[pl.BlockSpec((tm, d), lambda i: (i, 0)),
                   pl.BlockSpec((tm, LANES), lambda i: (i, 0)),
                   pl.BlockSpec((None, SUBLANES, tm), lambda i: (i, 0, 0)),
                   pl.BlockSpec((None, SUBLANES, LANES), lambda i: (i, 0, 0))],
        out_shape=[jax.ShapeDtypeStruct((t, d), BF16),
                   jax.ShapeDtypeStruct((t, LANES), F32),
                   jax.ShapeDtypeStruct((nt, SUBLANES, tm), F32),
                   jax.ShapeDtypeStruct((nt, SUBLANES, LANES), F32)],
        compiler_params=_cparams(("parallel",)),
        name="moe_router",
    )(x, gains, mod, mod, w_r, b_r)


def _moe_plan(k16, n_chunks):
    k = k16[:, 0, :N_EXPERTS].astype(I32)
    bpc = MOE_TMG // BAND
    tot = jnp.sum(k, axis=0)
    padc = ((tot + bpc - 1) // bpc) * bpc
    end_e = jnp.cumsum(padc)
    start_e = end_e - padc
    gstart = start_e[None, :] + jnp.cumsum(k, axis=0) - k
    lend = jnp.cumsum(k, axis=1)
    lstart = lend - k
    b = jnp.arange(MOE_MAXB, dtype=I32)
    mine = (b[None, :, None] >= lstart[:, None, :]) & (b[None, :, None] < lend[:, None, :])
    dstb = jnp.sum(jnp.where(mine, (gstart - lstart)[:, None, :], 0), axis=2) + b[None, :]
    nbands = lend[:, -1]
    dstb = jnp.where(b[None, :] < nbands[:, None], dstb, 0)
    cstart = jnp.arange(n_chunks, dtype=I32) * bpc
    cexp = jnp.sum((cstart[:, None] >= end_e[None, :]).astype(I32), axis=1)
    cexp = jnp.minimum(cexp, N_EXPERTS - 1)
    n_used = (end_e[-1] // bpc).reshape(1)
    n_tiles = k.shape[0]
    padb = _pad_bands_per_tile(n_tiles)
    total_bands = n_chunks * bpc
    gap_start = jnp.concatenate([start_e + tot, end_e[-1:]])
    gap_len = jnp.concatenate([padc - tot, total_bands - end_e[-1:]])
    pend = jnp.cumsum(gap_len)
    pstart = pend - gap_len
    j = jnp.arange(n_tiles * padb, dtype=I32)
    in_gap = (j[:, None] >= pstart[None, :]) & (j[:, None] < pend[None, :])
    dstp = jnp.sum(jnp.where(in_gap, (gap_start - pstart)[None, :] + j[:, None], 0), axis=1)
    npad = jnp.clip(pend[-1] - jnp.arange(n_tiles, dtype=I32) * padb, 0, padb)
    return dstb.reshape(-1), nbands, dstp, npad, cexp, n_used


def _pad_bands_per_tile(n_tiles):
    worst = (n_tiles * MOE_MAXR + N_EXPERTS * MOE_TMG - 2 * n_tiles * MOE_TM) // BAND
    return -(-worst // n_tiles)


def _band_copy_out(xloc, xs_ref, dstb_ref, sem, tile, b):
    slot = tile % 2
    src = xloc.at[slot, pl.ds(pl.multiple_of(b * BAND, BAND), BAND), :]
    row = pl.multiple_of(dstb_ref[tile * MOE_MAXB + b] * BAND, BAND)
    return pltpu.make_async_copy(src, xs_ref.at[pl.ds(row, BAND), :], sem.at[slot])


def _zero_band_copy(xloc, xs_ref, dstp_ref, sem, padb, tile, j):
    slot = tile % 2
    row = pl.multiple_of(dstp_ref[tile * padb + j] * BAND, BAND)
    return pltpu.make_async_copy(xloc.at[slot, pl.ds(MOE_MAXR, BAND), :],
                                 xs_ref.at[pl.ds(row, BAND), :], sem.at[slot])


def _dispatch_kernel(dstb_ref, nb_ref, dstp_ref, npad_ref, h_ref, lpt_ref, xs_ref, xloc, sem, *, padb):
    tile = pl.program_id(0)
    last = pl.num_programs(0) - 1
    tm, d = h_ref.shape

    def all_copies(t, op):
        def data(b, carry):
            op(_band_copy_out(xloc, xs_ref, dstb_ref, sem, t, b))
            return carry

        def zero(j, carry):
            op(_zero_band_copy(xloc, xs_ref, dstp_ref, sem, padb, t, j))
            return carry

        lax.fori_loop(0, nb_ref[t], data, 0)
        lax.fori_loop(0, npad_ref[t], zero, 0)

    @pl.when(tile >= 2)
    def _():
        all_copies(tile - 2, lambda cp: cp.wait())

    slot = tile % 2
    lp1 = lpt_ref[0:1, :]
    lp2 = lpt_ref[1:2, :]
    rows = lax.broadcasted_iota(I32, (MOE_MAXR + BAND, tm), 0).astype(F32)
    perm = jnp.where((rows == lp1) | (rows == lp2), 1.0, 0.0).astype(BF16)
    for c0 in range(0, d, MOE_COLS):
        xloc[slot, :, c0:c0 + MOE_COLS] = jnp.dot(
            perm, h_ref[:, c0:c0 + MOE_COLS], preferred_element_type=F32).astype(BF16)
    all_copies(tile, lambda cp: cp.start())

    @pl.when(tile == last)
    def _():
        @pl.when(tile >= 1)
        def _():
            all_copies(tile - 1, lambda cp: cp.wait())
        all_copies(tile, lambda cp: cp.wait())


def _dispatch_call(h, infot, dstb, nbands, dstp, npad, n_rows):
    t, d = h.shape
    nt = t // MOE_TM
    return pl.pallas_call(
        functools.partial(_dispatch_kernel, padb=_pad_bands_per_tile(nt)),
        grid_spec=pltpu.PrefetchScalarGridSpec(
            num_scalar_prefetch=4,
            grid=(nt,),
            in_specs=[pl.BlockSpec((MOE_TM, d), lambda i, *_: (i, 0)),
                      pl.BlockSpec((None, SUBLANES, MOE_TM), lambda i, *_: (i, 0, 0))],
            out_specs=pl.BlockSpec(memory_space=pl.ANY),
            scratch_shapes=[pltpu.VMEM((2, MOE_MAXR + BAND, d), BF16), pltpu.SemaphoreType.DMA((2,))]),
        out_shape=jax.ShapeDtypeStruct((n_rows, d), BF16),
        compiler_params=_cparams(("arbitrary",)),
        name="moe_dispatch",
    )(dstb, nbands, dstp, npad, h, infot)


def _experts_kernel(cexp_ref, nused_ref, x_ref, wg_ref, wu_ref, wd_ref, y_ref, wgu_ref):
    f = wd_ref.shape[0]
    c = pl.program_id(0)
    prev = cexp_ref[jnp.maximum(c - 1, 0)]

    @pl.when((c == 0) | (cexp_ref[c] != prev))
    def _():
        wgu_ref[:, 0:f] = wg_ref[...]
        wgu_ref[:, f:] = wu_ref[...]

    @pl.when(c < nused_ref[0])
    def _():
        half = x_ref.shape[0] // 2
        parts = [slice(0, half), slice(half, 2 * half)]
        gus = [jnp.dot(x_ref[rows, :], wgu_ref[...], preferred_element_type=F32) for rows in parts]
        for rows, gu in zip(parts, gus):
            g, u = gu[:, 0:f], gu[:, f:]
            act = ((g * jax.nn.sigmoid(g)) * u).astype(BF16)
            y_ref[rows, :] = jnp.dot(act, wd_ref[...], preferred_element_type=F32).astype(y_ref.dtype)

    @pl.when(c >= nused_ref[0])
    def _():
        y_ref[...] = jnp.zeros_like(y_ref)


def _experts_call(xs, wg, wu, wd, l, cexp, n_used):
    n_rows, d = xs.shape
    f = wd.shape[2]
    n_chunks = n_rows // MOE_TMG
    return pl.pallas_call(
        _experts_kernel,
        grid_spec=pltpu.PrefetchScalarGridSpec(
            num_scalar_prefetch=2,
            grid=(n_chunks,),
            in_specs=[pl.BlockSpec((MOE_TMG, d), lambda c, ce, nu: (jnp.minimum(c, nu[0] - 1), 0)),
                      pl.BlockSpec((None, None, d, f), lambda c, ce, nu: (l, ce[c], 0, 0)),
                      pl.BlockSpec((None, None, d, f), lambda c, ce, nu: (l, ce[c], 0, 0)),
                      pl.BlockSpec((None, None, f, d), lambda c, ce, nu: (l, ce[c], 0, 0))],
            out_specs=pl.BlockSpec((MOE_TMG, d), lambda c, ce, nu: (c, 0)),
            scratch_shapes=[pltpu.VMEM((d, 2 * f), BF16)]),
        out_shape=jax.ShapeDtypeStruct((n_rows, d), BF16),
        compiler_params=_cparams(("arbitrary",)),
        name="moe_experts",
    )(cexp, n_used, xs, wg, wu, wd)


def _band_copy_in(y_ref, yloc, dstb_ref, sem, tile, b):
    slot = tile % 2
    row = pl.multiple_of(dstb_ref[tile * MOE_MAXB + b] * BAND, BAND)
    dst = yloc.at[slot, pl.ds(pl.multiple_of(b * BAND, BAND), BAND), :]
    return pltpu.make_async_copy(y_ref.at[pl.ds(row, BAND), :], dst, sem.at[slot])


def _combine_kernel(dstb_ref, nb_ref, y_ref, info_ref, x_ref, g_ref, o_ref, yloc, sem):
    tile = pl.program_id(0)
    last = pl.num_programs(0) - 1
    tm, d = x_ref.shape

    def fetch(t):
        def start(b, carry):
            _band_copy_in(y_ref, yloc, dstb_ref, sem, t, b).start()
            return carry

        def clear(b, carry):
            yloc[t % 2, pl.ds(pl.multiple_of(b * BAND, BAND), BAND), :] = jnp.zeros((BAND, d), yloc.dtype)
            return carry

        lax.fori_loop(0, nb_ref[t], start, 0)
        lax.fori_loop(nb_ref[t], MOE_MAXB, clear, 0)

    @pl.when(tile == 0)
    def _():
        fetch(tile)

    @pl.when(tile < last)
    def _():
        fetch(tile + 1)

    info = info_ref[...]
    lp1, lp2, w1, w2 = info[:, 0:1], info[:, 1:2], info[:, 2:3], info[:, 3:4]
    cols = lax.broadcasted_iota(I32, (tm, MOE_MAXR), 1).astype(F32)
    pw = (jnp.where(cols == lp1, w1, 0.0) + jnp.where(cols == lp2, w2, 0.0)).astype(BF16)

    def wait(b, carry):
        _band_copy_in(y_ref, yloc, dstb_ref, sem, tile, b).wait()
        return carry

    lax.fori_loop(0, nb_ref[tile], wait, 0)
    slot = tile % 2
    for c0 in range(0, d, MOE_COLS):
        sl = slice(c0, c0 + MOE_COLS)
        acc = jnp.dot(pw, yloc[slot, :, sl], preferred_element_type=F32)
        o_ref[:, sl] = x_ref[:, sl] + g_ref[:, sl] * acc


def _combine_call(y, info, x, mod, l, gate_blk, dstb, nbands):
    t, d = x.shape
    nt = t // MOE_TM
    return pl.pallas_call(
        _combine_kernel,
        grid_spec=pltpu.PrefetchScalarGridSpec(
            num_scalar_prefetch=2,
            grid=(nt,),
            in_specs=[pl.BlockSpec(memory_space=pl.ANY),
                      pl.BlockSpec((MOE_TM, LANES), lambda i, *_: (i, 0)),
                      pl.BlockSpec((MOE_TM, d), lambda i, *_: (i, 0)),
                      pl.BlockSpec((None, 1, d), lambda i, *_: (l, 0, gate_blk))],
            out_specs=pl.BlockSpec((MOE_TM, d), lambda i, *_: (i, 0)),
            scratch_shapes=[pltpu.VMEM((2, MOE_MAXR, d), BF16), pltpu.SemaphoreType.DMA((2,))]),
        out_shape=jax.ShapeDtypeStruct((t, d), F32),
        compiler_params=_cparams(("arbitrary",)),
        name="moe_combine",
    )(dstb, nbands, y, info, x, mod)


def kernel(x, c, w_mod, mod_table, norm1_g, w_in, ssm_a_re, ssm_a_im, ssm_log_dt, ssm_b_re, ssm_b_im, ssm_c_re, ssm_c_im, ssm_d, w_glu, w_br_ssm, conv_w, w_br_conv, w_o, norm2_g, w_router_group, b_router_group, w_router_expert, b_router_expert, w_exp_gate, w_exp_up, w_exp_down, final_g):
    bsz, seq, d = x.shape
    depth = mod_table.shape[0]
    h_ssm = ssm_d.shape[1]
    h_conv = conv_w.shape[2]
    n_exp, _, d_expert = w_exp_gate.shape[1:]
    assert bsz == 1 and seq % SSM_TT == 0 and h_ssm % SSM_CB == 0 and seq % MOE_TM == 0
    assert ssm_a_re.shape[1:] == (h_ssm // SSM_GROUP, SSM_STATE) and n_exp == N_EXPERTS
    assert d % MOE_COLS == 0
    u_col, b_col, c_col, v_col = 0, h_ssm, h_ssm + h_conv, h_ssm + 2 * h_conv
    gs_col = h_ssm + 3 * h_conv
    sh1, sc1, g1, sh2, sc2, g2 = range(N_MOD)

    w_in_b = w_in.astype(BF16)
    w_glu_b = w_glu.astype(BF16)
    w_brs_b = w_br_ssm.astype(BF16)
    w_brc_b = w_br_conv.astype(BF16)
    w_o_b = w_o.astype(BF16)
    wg_b = w_exp_gate.astype(BF16)
    wu_b = w_exp_up.astype(BF16)
    wd_b = w_exp_down.astype(BF16)
    pad = LANES - N_GROUPS - N_EXPERTS
    w_r = jnp.concatenate([w_router_group, w_router_expert, jnp.zeros((depth, d, pad), F32)], axis=2).astype(BF16)
    b_r = jnp.concatenate([b_router_group, b_router_expert, jnp.zeros((depth, pad), F32)], axis=1)[:, None, :]
    s5_kc, s5_pc, s5_qc, ltab = _ssm_tables(ssm_a_re, ssm_a_im, ssm_log_dt, ssm_b_re, ssm_b_im,
                                           ssm_c_re, ssm_c_im)

    n_tiles = seq // MOE_TM
    n_rows = n_tiles * MOE_MAXR + N_EXPERTS * MOE_TMG
    n_chunks = n_rows // MOE_TMG

    xt = x.reshape(seq, d)
    mod = _mod_call(c, w_mod, mod_table)[:, None, :]
    norm1_g, norm2_g, ssm_d = norm1_g[:, None, :], norm2_g[:, None, :], ssm_d[:, None, :]

    for l in range(depth):
        h1 = _norm_mod_call(xt, norm1_g, mod, l, sc1, sh1)
        u = _mm_call(h1, w_in_b, l, u_col, h_ssm, False, F32, 1024, 512, "in_proj_u")
        yc = _proj_conv_call(h1, w_in_b, l, conv_w, b_col, c_col, v_col, h_conv)
        gates = _mm_call(h1, w_in_b, l, gs_col, 2 * d, True, BF16, 1024, 512, "in_proj_gates")
        y_pre = _ssm_call(u, 0, h_ssm, s5_kc, s5_pc, s5_qc, ltab, ssm_d, l)
        ys = _glu_call(y_pre, w_glu_b, l)
        merged = _merge_call(ys, yc, w_brs_b, w_brc_b, l, gates)
        xt = _resid_call(merged, w_o_b, l, xt, mod, g1, 1024, 512, "out_proj")

        h2, info, infot, k16 = _router_call(xt, norm2_g, mod, sc2, sh2, w_r, b_r, l)
        dstb, nbands, dstp, npad, cexp, n_used = _moe_plan(k16, n_chunks)
        xs = _dispatch_call(h2, infot, dstb, nbands, dstp, npad, n_rows)
        ye = _experts_call(xs, wg_b, wu_b, wd_b, l, cexp, n_used)
        xt = _combine_call(ye, info, xt, mod, l, g2, dstb, nbands)

    out = _norm_call(xt, final_g[None, :])
    return out.reshape(bsz, seq, d)
```

```python
import functools
import math

import jax
import jax.numpy as jnp
from jax import lax
from jax.experimental import pallas as pl
from jax.experimental.pallas import tpu as pltpu

F32 = jnp.float32
BF16 = jnp.bfloat16
I32 = jnp.int32

EPS = 1e-6
SSM_GROUP = 16
SSM_STATE = 64
CONV_WIDTH = 3
N_GROUPS = 4
EXPERTS_PER_GROUP = 4
N_EXPERTS = N_GROUPS * EXPERTS_PER_GROUP
N_MOD = 6

LANES = 128
SUBLANES = 8
VMEM_LIMIT_BYTES = 58 * 1024 * 1024

SSM_LC = 16
SSM_LA = 4
SSM_LB = 4
SSM_S2 = 16
SSM_K2 = 16
SSM_NK = SSM_S2 * SSM_K2
SSM_TT = SSM_NK * SSM_LC
SSM_PITCH = SSM_S2 * SSM_LC + SUBLANES
SSM_CB = 256
SSM_GB = SSM_CB // SSM_GROUP
SSM_NS = SSM_GB * SSM_STATE
LT_ROWS = 24
SSM_NPOW = 32

MOE_TM = 512
BAND = 2 * SUBLANES
MOE_MAXR = 2 * MOE_TM + N_EXPERTS * BAND
MOE_MAXB = MOE_MAXR // BAND
MOE_TMG = 512
MOE_COLS = 1024


def _cparams(sem):
    return pltpu.CompilerParams(dimension_semantics=sem, vmem_limit_bytes=VMEM_LIMIT_BYTES)


def _mod_kernel(c_ref, w_ref, t_ref, o_ref):
    c = c_ref[...]
    s = c * jax.nn.sigmoid(c)
    shared = jnp.sum(s * w_ref[...], axis=0, keepdims=True)
    o_ref[...] = t_ref[...] + shared


def _mod_call(c, w_mod, mod_table):
    d, n = w_mod.shape
    depth = mod_table.shape[0]
    tn = 512
    return pl.pallas_call(
        _mod_kernel,
        grid=(n // tn,),
        in_specs=[pl.BlockSpec((d, 1), lambda j: (0, 0)),
                  pl.BlockSpec((d, tn), lambda j: (0, j)),
                  pl.BlockSpec((depth, tn), lambda j: (0, j))],
        out_specs=pl.BlockSpec((depth, tn), lambda j: (0, j)),
        out_shape=jax.ShapeDtypeStruct((depth, n), F32),
        compiler_params=_cparams(("arbitrary",)),
        name="adaln_mod",
    )(c.reshape(d, 1), w_mod, mod_table)


def _norm_kernel(x_ref, g_ref, o_ref):
    x = x_ref[...]
    y = x * lax.rsqrt(jnp.mean(x * x, axis=-1, keepdims=True) + EPS)
    o_ref[...] = (y * g_ref[...]).astype(o_ref.dtype)


def _norm_call(x, g, tm=256):
    t, d = x.shape
    return pl.pallas_call(
        _norm_kernel,
        grid=(t // tm,),
        in_specs=[pl.BlockSpec((tm, d), lambda i: (i, 0)), pl.BlockSpec((1, d), lambda i: (0, 0))],
        out_specs=pl.BlockSpec((tm, d), lambda i: (i, 0)),
        out_shape=jax.ShapeDtypeStruct((t, d), F32),
        compiler_params=_cparams(("parallel",)),
        name="final_norm",
    )(x, g)


def _wspec(l, k, tn):
    return pl.BlockSpec((None, k, tn), lambda i, j: (l, 0, j))


def _norm_proj_kernel(x_ref, gain_ref, sc_ref, sh_ref, w_ref, h_ref, o_ref):
    @pl.when(pl.program_id(1) == 0)
    def _():
        x = x_ref[...]
        y = x * lax.rsqrt(jnp.mean(x * x, axis=-1, keepdims=True) + EPS)
        h_ref[...] = ((y * gain_ref[...]) * (1.0 + sc_ref[...]) + sh_ref[...]).astype(h_ref.dtype)

    o_ref[...] = jnp.dot(h_ref[...], w_ref[...], preferred_element_type=F32).astype(o_ref.dtype)


def _norm_proj_call(x, gains, mod, sc_blk, sh_blk, w, l, col0, n, tm=512, tn=512):
    t, d = x.shape
    cb = col0 // tn
    return pl.pallas_call(
        _norm_proj_kernel,
        grid=(t // tm, n // tn),
        in_specs=[pl.BlockSpec((tm, d), lambda i, j: (i, 0)),
                  pl.BlockSpec((None, 1, d), lambda i, j: (l, 0, 0)),
                  pl.BlockSpec((None, 1, d), lambda i, j: (l, 0, sc_blk)),
                  pl.BlockSpec((None, 1, d), lambda i, j: (l, 0, sh_blk)),
                  pl.BlockSpec((None, d, tn), lambda i, j: (l, 0, cb + j))],
        out_specs=[pl.BlockSpec((tm, d), lambda i, j: (i, 0)),
                   pl.BlockSpec((tm, tn), lambda i, j: (i, j))],
        out_shape=[jax.ShapeDtypeStruct((t, d), BF16), jax.ShapeDtypeStruct((t, n), F32)],
        compiler_params=_cparams(("parallel", "arbitrary")),
        name="norm_in_proj_u",
    )(x, gains, mod, mod, w)


def _mm_kernel(a_ref, w_ref, o_ref, *, gate):
    acc = jnp.dot(a_ref[...], w_ref[...], preferred_element_type=F32)
    o_ref[...] = (jax.nn.sigmoid(acc) if gate else acc).astype(o_ref.dtype)


def _mm_call(a, w, l, col0, n, gate, out_dtype, tm, tn, name):
    t, k = a.shape
    cb = col0 // tn
    return pl.pallas_call(
        functools.partial(_mm_kernel, gate=gate),
        grid=(t // tm, n // tn),
        in_specs=[pl.BlockSpec((tm, k), lambda i, j: (i, 0)),
                  pl.BlockSpec((None, k, tn), lambda i, j: (l, 0, cb + j))],
        out_specs=pl.BlockSpec((tm, tn), lambda i, j: (i, j)),
        out_shape=jax.ShapeDtypeStruct((t, n), out_dtype),
        compiler_params=_cparams(("parallel", "arbitrary")),
        name=name,
    )(a, w)


def _glu_kernel(a_ref, y_ref, w_ref, o_ref):
    acc = jnp.dot(a_ref[...], w_ref[...], preferred_element_type=F32)
    o_ref[...] = (y_ref[...].astype(F32) * jax.nn.sigmoid(acc)).astype(o_ref.dtype)


def _glu_call(y, w, l, tm=1024, tn=512):
    t, k = y.shape
    n = w.shape[2]
    return pl.pallas_call(
        _glu_kernel,
        grid=(t // tm, n // tn),
        in_specs=[pl.BlockSpec((tm, k), lambda i, j: (i, 0)),
                  pl.BlockSpec((tm, tn), lambda i, j: (i, j)),
                  _wspec(l, k, tn)],
        out_specs=pl.BlockSpec((tm, tn), lambda i, j: (i, j)),
        out_shape=jax.ShapeDtypeStruct((t, n), BF16),
        compiler_params=_cparams(("parallel", "arbitrary")),
        name="s5_glu",
    )(y, y, w)


def _merge_kernel(ys_ref, yc_ref, ws_ref, wc_ref, gs_ref, gc_ref, o_ref):
    y_ssm = jnp.dot(ys_ref[...], ws_ref[...], preferred_element_type=F32)
    y_conv = jnp.dot(yc_ref[...], wc_ref[...], preferred_element_type=F32)
    merged = gs_ref[...].astype(F32) * y_ssm + gc_ref[...].astype(F32) * y_conv
    o_ref[...] = merged.astype(o_ref.dtype)


def _merge_call(ys, yc, ws, wc, l, gates, tm=1024, tn=512):
    t, k = ys.shape
    n = ws.shape[2]
    gs_blk, gc_blk = 0, n // tn
    return pl.pallas_call(
        _merge_kernel,
        grid=(t // tm, n // tn),
        in_specs=[pl.BlockSpec((tm, k), lambda i, j: (i, 0)),
                  pl.BlockSpec((tm, k), lambda i, j: (i, 0)),
                  _wspec(l, k, tn),
                  _wspec(l, k, tn),
                  pl.BlockSpec((tm, tn), lambda i, j: (i, gs_blk + j)),
                  pl.BlockSpec((tm, tn), lambda i, j: (i, gc_blk + j))],
        out_specs=pl.BlockSpec((tm, tn), lambda i, j: (i, j)),
        out_shape=jax.ShapeDtypeStruct((t, n), BF16),
        compiler_params=_cparams(("parallel", "arbitrary")),
        name="branch_merge",
    )(ys, yc, ws, wc, gates, gates)


def _resid_kernel(a_ref, w_ref, x_ref, g_ref, o_ref):
    acc = jnp.dot(a_ref[...], w_ref[...], preferred_element_type=F32)
    o_ref[...] = x_ref[...] + g_ref[...] * acc


def _resid_call(a, w, l, x, mod, gate_blk, tm, tn, name):
    t, k = a.shape
    n = w.shape[2]
    gb = gate_blk * (n // tn)
    return pl.pallas_call(
        _resid_kernel,
        grid=(t // tm, n // tn),
        in_specs=[pl.BlockSpec((tm, k), lambda i, j: (i, 0)),
                  _wspec(l, k, tn),
                  pl.BlockSpec((tm, tn), lambda i, j: (i, j)),
                  pl.BlockSpec((None, 1, tn), lambda i, j: (l, 0, gb + j))],
        out_specs=pl.BlockSpec((tm, tn), lambda i, j: (i, j)),
        out_shape=jax.ShapeDtypeStruct((t, n), F32),
        compiler_params=_cparams(("parallel", "arbitrary")),
        name=name,
    )(a, w, x, mod)


def _proj_conv_kernel(a_ref, wb_ref, wc_ref, wv_ref, w_ref, o_ref, z_scr, carry):
    tm = a_ref.shape[0]
    halo = SUBLANES
    i, k = pl.program_id(0), pl.program_id(1)

    @pl.when((i == 0) & (k == 0))
    def _():
        carry[...] = jnp.zeros_like(carry)

    a = a_ref[...]
    bg = jnp.dot(a, wb_ref[...], preferred_element_type=F32)
    z = (jnp.dot(a, wc_ref[...], preferred_element_type=F32)
         * jnp.dot(a, wv_ref[...], preferred_element_type=F32))
    z_scr[0:halo, :] = carry[k]
    z_scr[halo:, :] = z
    carry[k] = z[tm - halo:, :]
    w = w_ref[...]
    conv = w[2:3, :] * z
    for tap in range(CONV_WIDTH - 1):
        conv = conv + w[tap:tap + 1, :] * z_scr[pl.ds(halo - (CONV_WIDTH - 1 - tap), tm), :]
    o_ref[...] = (bg * conv).astype(o_ref.dtype)


def _proj_conv_call(a, w, l, conv_w, b_col, c_col, v_col, width, tm=1024, tc=256):
    t, kdim = a.shape
    bb, cb, vb = b_col // tc, c_col // tc, v_col // tc

    def wspec(off):
        return pl.BlockSpec((None, kdim, tc), lambda i, k: (l, 0, off + k))

    return pl.pallas_call(
        _proj_conv_kernel,
        grid=(t // tm, width // tc),
        in_specs=[pl.BlockSpec((tm, kdim), lambda i, k: (i, 0)),
                  wspec(bb), wspec(cb), wspec(vb),
                  pl.BlockSpec((None, CONV_WIDTH, tc), lambda i, k: (l, 0, k))],
        out_specs=pl.BlockSpec((tm, tc), lambda i, k: (i, k)),
        out_shape=jax.ShapeDtypeStruct((t, width), BF16),
        scratch_shapes=[pltpu.VMEM((tm + SUBLANES, tc), F32),
                        pltpu.VMEM((width // tc, SUBLANES, tc), F32)],
        compiler_params=_cparams(("arbitrary", "arbitrary")),
        name="in_proj_conv",
    )(a, w, w, w, conv_w)


def _cmul_row(xr, xi, lr, li):
    return xr * lr - xi * li, xr * li + xi * lr


def _gelu_tanh(x):
    return 0.5 * x * (1.0 + jnp.tanh(math.sqrt(2.0 / math.pi) * (x + 0.044715 * (x * x * x))))


def _expand_block_diag(dst, spread_rows, row_shift, col_shift, chunk):
    n_r, n_c = dst.shape
    gmask = SSM_GB - 1
    for r0 in range(0, n_r, chunk):
        spread = spread_rows(r0, chunk)
        rg = (lax.broadcasted_iota(I32, (chunk, n_c), 0) + r0) >> row_shift
        cg = lax.broadcasted_iota(I32, (chunk, n_c), 1) >> col_shift
        keep = (rg & gmask) == (cg & gmask)
        dst[r0:r0 + chunk, :] = jnp.where(keep, spread, 0.0).astype(dst.dtype)


def _ssm_build_tables(pc_ref, qc_ref, kc_ref, p_scr, q_scr, k_scr):
    ns2 = 2 * SSM_NS
    p_state = SSM_STATE
    lp, lc = int(math.log2(p_state)), int(math.log2(SSM_GROUP))
    lns = int(math.log2(SSM_NS))
    r = lax.broadcasted_iota(I32, (LANES, ns2), 0)
    c = lax.broadcasted_iota(I32, (LANES, ns2), 1)
    tile_p = jnp.where(((r >> lp) == (c >> lns)) & ((r & (p_state - 1)) == (c & (p_state - 1))),
                       1.0, 0.0).astype(BF16)

    def p_rows(r0, chunk):
        return jnp.dot(pc_ref[r0:r0 + chunk, :], tile_p, preferred_element_type=F32)

    _expand_block_diag(p_scr, p_rows, lc, lp, 256)
    r = lax.broadcasted_iota(I32, (ns2, LANES), 0)
    c = lax.broadcasted_iota(I32, (ns2, LANES), 1)
    tile_q = jnp.where(((c >> lp) == (r >> lns)) & ((c & (p_state - 1)) == (r & (p_state - 1))),
                       1.0, 0.0).astype(BF16)

    def q_rows(r0, chunk):
        return jnp.dot(tile_q[r0:r0 + chunk, :], qc_ref[...], preferred_element_type=F32)

    _expand_block_diag(q_scr, q_rows, lp, lc, 512)
    r = lax.broadcasted_iota(I32, (LANES, SSM_CB), 0)
    c = lax.broadcasted_iota(I32, (LANES, SSM_CB), 1)
    tile_k = jnp.where(r == (c & (SSM_GROUP - 1)), 1.0, 0.0).astype(BF16)

    def k_rows(r0, chunk):
        return jnp.dot(kc_ref[r0:r0 + chunk, :], tile_k, preferred_element_type=F32)

    _expand_block_diag(k_scr, k_rows, lc, lc, 1024)


def _ssm_kernel(u0_ref, u1_ref, kc_ref, pc_ref, qc_ref, lt_ref, d_ref, o_ref,
                ub2, y2, bx, s_scr, ha, cs, carry, yo0, yo1, kall_ref, p_ref, q_ref, up0, up1):
    ns = SSM_NS
    cb = SSM_CB
    rows = SSM_K2
    ksteps = SSM_S2 * SSM_LC
    kstride = SSM_PITCH
    u_halves = (up0, up1)
    yo = (yo0, yo1)

    for src, dst in ((u0_ref, up0), (u1_ref, up1)):
        for k2 in range(SSM_K2):
            dst[k2 * kstride:k2 * kstride + ksteps, :] = src[k2 * ksteps:(k2 + 1) * ksteps, :]

    @pl.when(pl.program_id(1) == 0)
    def _():
        carry[...] = jnp.zeros_like(carry)
        _ssm_build_tables(pc_ref, qc_ref, kc_ref, p_ref, q_ref, kall_ref)

    def lt(row):
        return lt_ref[row:row + 1, 0:ns], lt_ref[row:row + 1, ns:2 * ns]

    for s in range(SSM_LC):
        for s2 in range(SSM_S2):
            idx = pl.ds(s2 * SSM_LC + s, rows, stride=kstride)
            for hf, u_ref in enumerate(u_halves):
                c0 = s * cb + hf * LANES
                ub2[s2 * rows:(s2 + 1) * rows, c0:c0 + LANES] = u_ref[idx, :].astype(BF16)

    for sp in range(SSM_LC):
        y2[:, sp * cb:(sp + 1) * cb] = jnp.dot(
            ub2[:, 0:(sp + 1) * cb], kall_ref[(SSM_LC - 1 - sp) * cb:, :],
            preferred_element_type=F32)

    s_scr[...] = jnp.dot(ub2[:, (SSM_LA - 1) * SSM_LB * cb:], p_ref[...], preferred_element_type=F32)
    for a in range(SSM_LA - 1):
        bx[...] = jnp.dot(ub2[:, a * SSM_LB * cb:(a + 1) * SSM_LB * cb], p_ref[...],
                          preferred_element_type=F32)
        lr, li = lt(SSM_LA - 2 - a)
        for s2 in range(SSM_S2):
            sl = slice(s2 * rows, (s2 + 1) * rows)
            pr, pi = _cmul_row(bx[sl, 0:ns], bx[sl, ns:], lr, li)
            s_scr[sl, 0:ns] += pr
            s_scr[sl, ns:] += pi

    lr, li = lt(4 + 1)
    for s2 in range(1, SSM_S2):
        pv = slice((s2 - 1) * rows, s2 * rows)
        sl = slice(s2 * rows, (s2 + 1) * rows)
        pr, pi = _cmul_row(s_scr[pv, 0:ns], s_scr[pv, ns:], lr, li)
        s_scr[sl, 0:ns] += pr
        s_scr[sl, ns:] += pi

    lr, li = lt(3)
    last = slice((SSM_S2 - 1) * rows, SSM_S2 * rows)
    zr, zi = s_scr[last, 0:ns], s_scr[last, ns:]
    cr, ci = carry[0:1, 0:ns], carry[0:1, ns:]
    for k2 in range(SSM_K2):
        cs[k2:k2 + 1, 0:ns] = cr
        cs[k2:k2 + 1, ns:] = ci
        pr, pi = _cmul_row(cr, ci, lr, li)
        cr, ci = pr + zr[k2:k2 + 1, :], pi + zi[k2:k2 + 1, :]
    carry[0:1, 0:ns] = cr
    carry[0:1, ns:] = ci

    for a in range(SSM_LA):
        for s2 in range(SSM_S2):
            sl = slice(s2 * rows, (s2 + 1) * rows)
            hr, hi = cs[:, 0:ns], cs[:, ns:]
            if s2 > 0:
                pv = slice((s2 - 1) * rows, s2 * rows)
                lr, li = lt(4 + s2)
                pr, pi = _cmul_row(hr, hi, lr, li)
                hr, hi = s_scr[pv, 0:ns] + pr, s_scr[pv, ns:] + pi
            if a > 0:
                lr, li = lt(a - 1)
                hr, hi = _cmul_row(hr, hi, lr, li)
            ha[sl, 0:ns] = hr.astype(BF16)
            ha[sl, ns:] = hi.astype(BF16)
        wa = SSM_LB * cb
        y2[:, a * wa:(a + 1) * wa] += jnp.dot(ha[...], q_ref[...], preferred_element_type=F32)

    d = d_ref[...]
    for s in range(SSM_LC):
        for s2 in range(SSM_S2):
            idx = pl.ds(s2 * SSM_LC + s, rows, stride=kstride)
            for hf, u_ref in enumerate(u_halves):
                c0 = s * cb + hf * LANES
                y = (y2[s2 * rows:(s2 + 1) * rows, c0:c0 + LANES]
                     + d[:, hf * LANES:(hf + 1) * LANES] * u_ref[idx, :])
                yo[hf][idx, :] = _gelu_tanh(y)
    for hf in range(len(u_halves)):
        for k2 in range(SSM_K2):
            o_ref[k2 * ksteps:(k2 + 1) * ksteps, hf * LANES:(hf + 1) * LANES] = (
                yo[hf][k2 * kstride:k2 * kstride + ksteps, :].astype(o_ref.dtype))


def _ssm_call(proj, u_col, h_ssm, kc, pc, qc, ltab, d_skip, l):
    t = proj.shape[0]
    nb = h_ssm // SSM_CB
    ub = u_col // SSM_CB
    ns2 = 2 * SSM_NS

    def tab(r, c):
        return pl.BlockSpec((None, None, r, c), lambda b, i: (l, b, 0, 0))

    return pl.pallas_call(
        _ssm_kernel,
        grid=(nb, t // SSM_TT),
        in_specs=[pl.BlockSpec((SSM_TT, LANES), lambda b, i: (i, 2 * (ub + b))),
                  pl.BlockSpec((SSM_TT, LANES), lambda b, i: (i, 2 * (ub + b) + 1)),
                  tab(SSM_LC * SSM_CB, LANES),
                  tab(SSM_LB * SSM_CB, LANES),
                  tab(LANES, SSM_LB * SSM_CB),
                  tab(LT_ROWS, ns2),
                  pl.BlockSpec((None, 1, SSM_CB), lambda b, i: (l, 0, b))],
        out_specs=pl.BlockSpec((SSM_TT, SSM_CB), lambda b, i: (i, b)),
        out_shape=jax.ShapeDtypeStruct((t, h_ssm), BF16),
        scratch_shapes=[pltpu.VMEM((SSM_NK, SSM_LC * SSM_CB), BF16),
                        pltpu.VMEM((SSM_NK, SSM_LC * SSM_CB), F32),
                        pltpu.VMEM((SSM_NK, ns2), F32),
                        pltpu.VMEM((SSM_NK, ns2), F32),
                        pltpu.VMEM((SSM_NK, ns2), BF16),
                        pltpu.VMEM((SSM_K2, ns2), F32),
                        pltpu.VMEM((SUBLANES, ns2), F32),
                        pltpu.VMEM((SSM_K2 * SSM_PITCH, LANES), F32),
                        pltpu.VMEM((SSM_K2 * SSM_PITCH, LANES), F32),
                        pltpu.VMEM((SSM_LC * SSM_CB, SSM_CB), BF16),
                        pltpu.VMEM((SSM_LB * SSM_CB, ns2), BF16),
                        pltpu.VMEM((ns2, SSM_LB * SSM_CB), BF16),
                        pltpu.VMEM((SSM_K2 * SSM_PITCH, LANES), F32),
                        pltpu.VMEM((SSM_K2 * SSM_PITCH, LANES), F32)],
        compiler_params=_cparams(("arbitrary", "arbitrary")),
        name="s5_scan",
    )(proj, proj, kc, pc, qc, ltab, d_skip)


def _ssm_tables(a_re, a_im, log_dt, b_re, b_im, c_re, c_im):
    assert 2 * a_re.shape[2] == LANES
    nl, g, p = a_re.shape
    gc = SSM_GROUP
    gb = SSM_GB
    nb = g // gb
    hi = lax.Precision.HIGHEST
    delta = jnp.exp(log_dt)[..., None]
    mag = jnp.exp(delta * a_re)
    ang = delta * a_im
    lam = (mag * jnp.cos(ang), mag * jnp.sin(ang))
    den = a_re * a_re + a_im * a_im
    zr, zi = lam[0] - 1.0, lam[1]
    coef_re = (zr * a_re + zi * a_im) / den
    coef_im = (zi * a_re - zr * a_im) / den
    bt_re = coef_re[..., None] * b_re - coef_im[..., None] * b_im
    bt_im = coef_re[..., None] * b_im + coef_im[..., None] * b_re

    def cmul(x, y):
        return x[0] * y[0] - x[1] * y[1], x[0] * y[1] + x[1] * y[0]

    def power_table(base):
        tab = (jnp.stack([jnp.ones_like(base[0]), base[0]]), jnp.stack([jnp.zeros_like(base[1]), base[1]]))
        step = cmul(base, base)
        while tab[0].shape[0] < SSM_NPOW:
            nxt = cmul(tab, (step[0][None], step[1][None]))
            tab = (jnp.concatenate([tab[0], nxt[0]]), jnp.concatenate([tab[1], nxt[1]]))
            step = cmul(step, step)
        return tab

    pw = power_table(lam)
    pw16 = power_table((pw[0][SSM_LC], pw[1][SSM_LC]))

    wr = pw[0][:SSM_LC, ..., None] * bt_re[None] - pw[1][:SSM_LC, ..., None] * bt_im[None]
    wi = pw[0][:SSM_LC, ..., None] * bt_im[None] + pw[1][:SSM_LC, ..., None] * bt_re[None]

    k = (jnp.einsum("lgop,tlgpi->tlgio", c_re, wr, precision=hi)
         - jnp.einsum("lgop,tlgpi->tlgio", c_im, wi, precision=hi))
    k = k[::-1].reshape(SSM_LC, nl, nb, gb, gc, gc).transpose(1, 2, 0, 3, 4, 5)
    kc = k.reshape(nl, nb, SSM_LC * SSM_CB, gc)
    kc = jnp.pad(kc, ((0, 0), (0, 0), (0, 0), (0, LANES - gc))).astype(BF16)

    def p_rows(w):
        return w[SSM_LB - 1::-1].reshape(SSM_LB, nl, nb, gb, p, gc).transpose(1, 2, 0, 3, 5, 4)

    pc = jnp.stack([p_rows(wr), p_rows(wi)], axis=5)
    pc = pc.reshape(nl, nb, SSM_LB * SSM_CB, 2 * p).astype(BF16)

    qw = (pw[0][1:SSM_LB + 1, :, :, None, :], pw[1][1:SSM_LB + 1, :, :, None, :])
    dr = c_re[None] * qw[0] - c_im[None] * qw[1]
    di = c_re[None] * qw[1] + c_im[None] * qw[0]

    def q_cols(w):
        return w.reshape(SSM_LB, nl, nb, gb, gc, p).transpose(1, 2, 5, 0, 3, 4)

    qc = jnp.stack([q_cols(dr), q_cols(-di)], axis=2)
    qc = qc.reshape(nl, nb, 2 * p, SSM_LB * SSM_CB).astype(BF16)

    def lrows(part):
        r = jnp.concatenate([pw[part][SSM_LB:SSM_LC:SSM_LB], pw16[part][SSM_S2:SSM_S2 + 1],
                             pw16[part][:SSM_S2]])
        r = jnp.concatenate([r, jnp.zeros((LT_ROWS - r.shape[0],) + r.shape[1:], F32)])
        return r.reshape(LT_ROWS, nl, nb, SSM_NS).transpose(1, 2, 0, 3)

    ltab = jnp.concatenate([lrows(0), lrows(1)], axis=3)
    return kc, pc, qc, ltab


def _router_kernel(x_ref, gain_ref, sc_ref, sh_ref, w_ref, b_ref, h_ref, info_ref, infot_ref, k16_ref):
    tm = x_ref.shape[0]
    x = x_ref[...]
    y = x * lax.rsqrt(jnp.mean(x * x, axis=-1, keepdims=True) + EPS)
    h = ((y * gain_ref[...]) * (1.0 + sc_ref[...]) + sh_ref[...]).astype(h_ref.dtype)
    h_ref[...] = h
    lg = jnp.dot(h, w_ref[...], preferred_element_type=F32) + b_ref[...]
    col = lax.broadcasted_iota(I32, lg.shape, 1)
    big = jnp.int32(LANES)
    neg = -jnp.inf
    is_g = col < N_GROUPS
    lgg = jnp.where(is_g, lg, neg)
    gmax = jnp.max(lgg, axis=1, keepdims=True)
    gsel = jnp.min(jnp.where(is_g & (lgg == gmax), col, big), axis=1, keepdims=True)
    denom = jnp.sum(jnp.where(is_g, jnp.exp(lgg - gmax), 0.0), axis=1, keepdims=True)
    p_sel = 1.0 / denom
    ecol = col - N_GROUPS
    egrp = lax.shift_right_arithmetic(ecol, jnp.int32(int(math.log2(EXPERTS_PER_GROUP))))
    in_grp = (ecol >= 0) & (ecol < N_EXPERTS) & (egrp == gsel)
    le = jnp.where(in_grp, lg, neg)
    m1 = jnp.max(le, axis=1, keepdims=True)
    i1 = jnp.min(jnp.where(in_grp & (le == m1), col, big), axis=1, keepdims=True)
    rest = in_grp & (col != i1)
    le2 = jnp.where(rest, lg, neg)
    m2 = jnp.max(le2, axis=1, keepdims=True)
    i2 = jnp.min(jnp.where(rest & (le2 == m2), col, big), axis=1, keepdims=True)
    e2x = jnp.exp(m2 - m1)
    w1 = p_sel / (1.0 + e2x)
    w2 = p_sel * e2x / (1.0 + e2x)

    oh1 = col == (i1 - N_GROUPS)
    oh2 = col == (i2 - N_GROUPS)
    oh = jnp.where(oh1 | oh2, 1.0, 0.0)
    r_i = lax.broadcasted_iota(I32, (tm, tm), 0)
    c_i = lax.broadcasted_iota(I32, (tm, tm), 1)
    earlier = jnp.where(r_i > c_i, 1.0, 0.0).astype(BF16)
    before = jnp.dot(earlier, oh.astype(BF16), preferred_element_type=F32)
    cnt = jnp.sum(oh, axis=0, keepdims=True)
    k16 = jnp.floor((cnt + (BAND - 1.0)) * (1.0 / BAND))
    k16_8 = jnp.broadcast_to(k16, (SUBLANES, LANES))
    e_r = lax.broadcasted_iota(I32, (LANES, LANES), 0)
    e_c = lax.broadcasted_iota(I32, (LANES, LANES), 1)
    lower_e = jnp.where(e_r < e_c, 1.0, 0.0).astype(BF16)
    seg_start = jnp.dot(k16_8.astype(BF16), lower_e, preferred_element_type=F32)[0:1, :] * BAND
    base = before + seg_start
    lp1 = jnp.sum(jnp.where(oh1, base, 0.0), axis=1, keepdims=True)
    lp2 = jnp.sum(jnp.where(oh2, base, 0.0), axis=1, keepdims=True)

    info = jnp.where(col == 0, lp1, jnp.where(col == 1, lp2, jnp.where(col == 2, w1, jnp.where(col == 3, w2, 0.0))))
    info_ref[...] = info
    infot_ref[...] = info.T[0:SUBLANES, :]
    k16_ref[...] = k16_8


def _router_call(x, gains, mod, sc_blk, sh_blk, w_r, b_r, l):
    t, d = x.shape
    tm = MOE_TM
    nt = t // tm
    return pl.pallas_call(
        _router_kernel,
        grid=(nt,),
        in_specs=[pl.BlockSpec((tm, d), lambda i: (i, 0)),
                  pl.BlockSpec((None, 1, d), lambda i: (l, 0, 0)),
                  pl.BlockSpec((None, 1, d), lambda i: (l, 0, sc_blk)),
                  pl.BlockSpec((None, 1, d), lambda i: (l, 0, sh_blk)),
                  pl.BlockSpec((None, d, LANES), lambda i: (l, 0, 0)),
                  pl.BlockSpec((None, 1, LANES), lambda i: (l, 0, 0))],
        out_specs=[pl.BlockSpec((tm, d), lambda i: (i, 0)),
                   pl.BlockSpec((tm, LANES), lambda i: (i, 0)),
                   pl.BlockSpec((None, SUBLANES, tm), lambda i: (i, 0, 0)),
                   pl.BlockSpec((None, SUBLANES, LANES), lambda i: (i, 0, 0))],
        out_shape=[jax.ShapeDtypeStruct((t, d), BF16),
                   jax.ShapeDtypeStruct((t, LANES), F32),
                   jax.ShapeDtypeStruct((nt, SUBLANES, tm), F32),
                   jax.ShapeDtypeStruct((nt, SUBLANES, LANES), F32)],
        compiler_params=_cparams(("parallel",)),
        name="moe_router",
    )(x, gains, mod, mod, w_r, b_r)


def _moe_plan(k16, n_chunks):
    k = k16[:, 0, :N_EXPERTS].astype(I32)
    bpc = MOE_TMG // BAND
    tot = jnp.sum(k, axis=0)
    padc = ((tot + bpc - 1) // bpc) * bpc
    end_e = jnp.cumsum(padc)
    start_e = end_e - padc
    gstart = start_e[None, :] + jnp.cumsum(k, axis=0) - k
    lend = jnp.cumsum(k, axis=1)
    lstart = lend - k
    b = jnp.arange(MOE_MAXB, dtype=I32)
    mine = (b[None, :, None] >= lstart[:, None, :]) & (b[None, :, None] < lend[:, None, :])
    dstb = jnp.sum(jnp.where(mine, (gstart - lstart)[:, None, :], 0), axis=2) + b[None, :]
    nbands = lend[:, -1]
    dstb = jnp.where(b[None, :] < nbands[:, None], dstb, 0)
    cstart = jnp.arange(n_chunks, dtype=I32) * bpc
    cexp = jnp.sum((cstart[:, None] >= end_e[None, :]).astype(I32), axis=1)
    cexp = jnp.minimum(cexp, N_EXPERTS - 1)
    n_used = (end_e[-1] // bpc).reshape(1)
    n_tiles = k.shape[0]
    padb = _pad_bands_per_tile(n_tiles)
    total_bands = n_chunks * bpc
    gap_start = jnp.concatenate([start_e + tot, end_e[-1:]])
    gap_len = jnp.concatenate([padc - tot, total_bands - end_e[-1:]])
    pend = jnp.cumsum(gap_len)
    pstart = pend - gap_len
    j = jnp.arange(n_tiles * padb, dtype=I32)
    in_gap = (j[:, None] >= pstart[None, :]) & (j[:, None] < pend[None, :])
    dstp = jnp.sum(jnp.where(in_gap, (gap_start - pstart)[None, :] + j[:, None], 0), axis=1)
    npad = jnp.clip(pend[-1] - jnp.arange(n_tiles, dtype=I32) * padb, 0, padb)
    return dstb.reshape(-1), nbands, dstp, npad, cexp, n_used


def _pad_bands_per_tile(n_tiles):
    worst = (n_tiles * MOE_MAXR + N_EXPERTS * MOE_TMG - 2 * n_tiles * MOE_TM) // BAND
    return -(-worst // n_tiles)


def _band_copy_out(xloc, xs_ref, dstb_ref, sem, tile, b):
    slot = tile % 2
    src = xloc.at[slot, pl.ds(pl.multiple_of(b * BAND, BAND), BAND), :]
    row = pl.multiple_of(dstb_ref[tile * MOE_MAXB + b] * BAND, BAND)
    return pltpu.make_async_copy(src, xs_ref.at[pl.ds(row, BAND), :], sem.at[slot])


def _zero_band_copy(xloc, xs_ref, dstp_ref, sem, padb, tile, j):
    slot = tile % 2
    row = pl.multiple_of(dstp_ref[tile * padb + j] * BAND, BAND)
    return pltpu.make_async_copy(xloc.at[slot, pl.ds(MOE_MAXR, BAND), :],
                                 xs_ref.at[pl.ds(row, BAND), :], sem.at[slot])


def _dispatch_kernel(dstb_ref, nb_ref, dstp_ref, npad_ref, h_ref, lpt_ref, xs_ref, xloc, sem, *, padb):
    tile = pl.program_id(0)
    last = pl.num_programs(0) - 1
    tm, d = h_ref.shape

    def all_copies(t, op):
        def data(b, carry):
            op(_band_copy_out(xloc, xs_ref, dstb_ref, sem, t, b))
            return carry

        def zero(j, carry):
            op(_zero_band_copy(xloc, xs_ref, dstp_ref, sem, padb, t, j))
            return carry

        lax.fori_loop(0, nb_ref[t], data, 0)
        lax.fori_loop(0, npad_ref[t], zero, 0)

    @pl.when(tile >= 2)
    def _():
        all_copies(tile - 2, lambda cp: cp.wait())

    slot = tile % 2
    lp1 = lpt_ref[0:1, :]
    lp2 = lpt_ref[1:2, :]
    rows = lax.broadcasted_iota(I32, (MOE_MAXR + BAND, tm), 0).astype(F32)
    perm = jnp.where((rows == lp1) | (rows == lp2), 1.0, 0.0).astype(BF16)
    for c0 in range(0, d, MOE_COLS):
        xloc[slot, :, c0:c0 + MOE_COLS] = jnp.dot(
            perm, h_ref[:, c0:c0 + MOE_COLS], preferred_element_type=F32).astype(BF16)
    all_copies(tile, lambda cp: cp.start())

    @pl.when(tile == last)
    def _():
        @pl.when(tile >= 1)
        def _():
            all_copies(tile - 1, lambda cp: cp.wait())
        all_copies(tile, lambda cp: cp.wait())


def _dispatch_call(h, infot, dstb, nbands, dstp, npad, n_rows):
    t, d = h.shape
    nt = t // MOE_TM
    return pl.pallas_call(
        functools.partial(_dispatch_kernel, padb=_pad_bands_per_tile(nt)),
        grid_spec=pltpu.PrefetchScalarGridSpec(
            num_scalar_prefetch=4,
            grid=(nt,),
            in_specs=[pl.BlockSpec((MOE_TM, d), lambda i, *_: (i, 0)),
                      pl.BlockSpec((None, SUBLANES, MOE_TM), lambda i, *_: (i, 0, 0))],
            out_specs=pl.BlockSpec(memory_space=pl.ANY),
            scratch_shapes=[pltpu.VMEM((2, MOE_MAXR + BAND, d), BF16), pltpu.SemaphoreType.DMA((2,))]),
        out_shape=jax.ShapeDtypeStruct((n_rows, d), BF16),
        compiler_params=_cparams(("arbitrary",)),
        name="moe_dispatch",
    )(dstb, nbands, dstp, npad, h, infot)


def _experts_kernel(cexp_ref, nused_ref, x_ref, wg_ref, wu_ref, wd_ref, y_ref, wgu_ref):
    f = wd_ref.shape[0]
    c = pl.program_id(0)
    prev = cexp_ref[jnp.maximum(c - 1, 0)]

    @pl.when((c == 0) | (cexp_ref[c] != prev))
    def _():
        wgu_ref[:, 0:f] = wg_ref[...]
        wgu_ref[:, f:] = wu_ref[...]

    @pl.when(c < nused_ref[0])
    def _():
        half = x_ref.shape[0] // 2
        parts = [slice(0, half), slice(half, 2 * half)]
        gus = [jnp.dot(x_ref[rows, :], wgu_ref[...], preferred_element_type=F32) for rows in parts]
        for rows, gu in zip(parts, gus):
            g, u = gu[:, 0:f], gu[:, f:]
            act = ((g * jax.nn.sigmoid(g)) * u).astype(BF16)
            y_ref[rows, :] = jnp.dot(act, wd_ref[...], preferred_element_type=F32).astype(y_ref.dtype)

    @pl.when(c >= nused_ref[0])
    def _():
        y_ref[...] = jnp.zeros_like(y_ref)


def _experts_call(xs, wg, wu, wd, l, cexp, n_used):
    n_rows, d = xs.shape
    f = wd.shape[2]
    n_chunks = n_rows // MOE_TMG
    return pl.pallas_call(
        _experts_kernel,
        grid_spec=pltpu.PrefetchScalarGridSpec(
            num_scalar_prefetch=2,
            grid=(n_chunks,),
            in_specs=[pl.BlockSpec((MOE_TMG, d), lambda c, ce, nu: (jnp.minimum(c, nu[0] - 1), 0)),
                      pl.BlockSpec((None, None, d, f), lambda c, ce, nu: (l, ce[c], 0, 0)),
                      pl.BlockSpec((None, None, d, f), lambda c, ce, nu: (l, ce[c], 0, 0)),
                      pl.BlockSpec((None, None, f, d), lambda c, ce, nu: (l, ce[c], 0, 0))],
            out_specs=pl.BlockSpec((MOE_TMG, d), lambda c, ce, nu: (c, 0)),
            scratch_shapes=[pltpu.VMEM((d, 2 * f), BF16)]),
        out_shape=jax.ShapeDtypeStruct((n_rows, d), BF16),
        compiler_params=_cparams(("arbitrary",)),
        name="moe_experts",
    )(cexp, n_used, xs, wg, wu, wd)


def _band_copy_in(y_ref, yloc, dstb_ref, sem, tile, b):
    slot = tile % 2
    row = pl.multiple_of(dstb_ref[tile * MOE_MAXB + b] * BAND, BAND)
    dst = yloc.at[slot, pl.ds(pl.multiple_of(b * BAND, BAND), BAND), :]
    return pltpu.make_async_copy(y_ref.at[pl.ds(row, BAND), :], dst, sem.at[slot])


def _combine_kernel(dstb_ref, nb_ref, y_ref, info_ref, x_ref, g_ref, o_ref, yloc, sem):
    tile = pl.program_id(0)
    last = pl.num_programs(0) - 1
    tm, d = x_ref.shape

    def fetch(t):
        def start(b, carry):
            _band_copy_in(y_ref, yloc, dstb_ref, sem, t, b).start()
            return carry

        def clear(b, carry):
            yloc[t % 2, pl.ds(pl.multiple_of(b * BAND, BAND), BAND), :] = jnp.zeros((BAND, d), yloc.dtype)
            return carry

        lax.fori_loop(0, nb_ref[t], start, 0)
        lax.fori_loop(nb_ref[t], MOE_MAXB, clear, 0)

    @pl.when(tile == 0)
    def _():
        fetch(tile)

    @pl.when(tile < last)
    def _():
        fetch(tile + 1)

    info = info_ref[...]
    lp1, lp2, w1, w2 = info[:, 0:1], info[:, 1:2], info[:, 2:3], info[:, 3:4]
    cols = lax.broadcasted_iota(I32, (tm, MOE_MAXR), 1).astype(F32)
    pw = (jnp.where(cols == lp1, w1, 0.0) + jnp.where(cols == lp2, w2, 0.0)).astype(BF16)

    def wait(b, carry):
        _band_copy_in(y_ref, yloc, dstb_ref, sem, tile, b).wait()
        return carry

    lax.fori_loop(0, nb_ref[tile], wait, 0)
    slot = tile % 2
    for c0 in range(0, d, MOE_COLS):
        sl = slice(c0, c0 + MOE_COLS)
        acc = jnp.dot(pw, yloc[slot, :, sl], preferred_element_type=F32)
        o_ref[:, sl] = x_ref[:, sl] + g_ref[:, sl] * acc


def _combine_call(y, info, x, mod, l, gate_blk, dstb, nbands):
    t, d = x.shape
    nt = t // MOE_TM
    return pl.pallas_call(
        _combine_kernel,
        grid_spec=pltpu.PrefetchScalarGridSpec(
            num_scalar_prefetch=2,
            grid=(nt,),
            in_specs=[pl.BlockSpec(memory_space=pl.ANY),
                      pl.BlockSpec((MOE_TM, LANES), lambda i, *_: (i, 0)),
                      pl.BlockSpec((MOE_TM, d), lambda i, *_: (i, 0)),
                      pl.BlockSpec((None, 1, d), lambda i, *_: (l, 0, gate_blk))],
            out_specs=pl.BlockSpec((MOE_TM, d), lambda i, *_: (i, 0)),
            scratch_shapes=[pltpu.VMEM((2, MOE_MAXR, d), BF16), pltpu.SemaphoreType.DMA((2,))]),
        out_shape=jax.ShapeDtypeStruct((t, d), F32),
        compiler_params=_cparams(("arbitrary",)),
        name="moe_combine",
    )(dstb, nbands, y, info, x, mod)


def kernel(x, c, w_mod, mod_table, norm1_g, w_in, ssm_a_re, ssm_a_im, ssm_log_dt, ssm_b_re, ssm_b_im, ssm_c_re, ssm_c_im, ssm_d, w_glu, w_br_ssm, conv_w, w_br_conv, w_o, norm2_g, w_router_group, b_router_group, w_router_expert, b_router_expert, w_exp_gate, w_exp_up, w_exp_down, final_g):
    bsz, seq, d = x.shape
    depth = mod_table.shape[0]
    h_ssm = ssm_d.shape[1]
    h_conv = conv_w.shape[2]
    n_exp, _, d_expert = w_exp_gate.shape[1:]
    assert bsz == 1 and seq % SSM_TT == 0 and h_ssm % SSM_CB == 0 and seq % MOE_TM == 0
    assert ssm_a_re.shape[1:] == (h_ssm // SSM_GROUP, SSM_STATE) and n_exp == N_EXPERTS
    assert d % MOE_COLS == 0
    u_col, b_col, c_col, v_col = 0, h_ssm, h_ssm + h_conv, h_ssm + 2 * h_conv
    gs_col = h_ssm + 3 * h_conv
    sh1, sc1, g1, sh2, sc2, g2 = range(N_MOD)

    w_in_b = w_in.astype(BF16)
    w_glu_b = w_glu.astype(BF16)
    w_brs_b = w_br_ssm.astype(BF16)
    w_brc_b = w_br_conv.astype(BF16)
    w_o_b = w_o.astype(BF16)
    wg_b = w_exp_gate.astype(BF16)
    wu_b = w_exp_up.astype(BF16)
    wd_b = w_exp_down.astype(BF16)
    pad = LANES - N_GROUPS - N_EXPERTS
    w_r = jnp.concatenate([w_router_group, w_router_expert, jnp.zeros((depth, d, pad), F32)], axis=2).astype(BF16)
    b_r = jnp.concatenate([b_router_group, b_router_expert, jnp.zeros((depth, pad), F32)], axis=1)[:, None, :]
    s5_kc, s5_pc, s5_qc, ltab = _ssm_tables(ssm_a_re, ssm_a_im, ssm_log_dt, ssm_b_re, ssm_b_im,
                                           ssm_c_re, ssm_c_im)

    n_tiles = seq // MOE_TM
    n_rows = n_tiles * MOE_MAXR + N_EXPERTS * MOE_TMG
    n_chunks = n_rows // MOE_TMG

    xt = x.reshape(seq, d)
    mod = _mod_call(c, w_mod, mod_table)[:, None, :]
    norm1_g, norm2_g, ssm_d = norm1_g[:, None, :], norm2_g[:, None, :], ssm_d[:, None, :]

    for l in range(depth):
        h1, u = _norm_proj_call(xt, norm1_g, mod, sc1, sh1, w_in_b, l, u_col, h_ssm)
        yc = _proj_conv_call(h1, w_in_b, l, conv_w, b_col, c_col, v_col, h_conv)
        gates = _mm_call(h1, w_in_b, l, gs_col, 2 * d, True, BF16, 1024, 512, "in_proj_gates")
        y_pre = _ssm_call(u, 0, h_ssm, s5_kc, s5_pc, s5_qc, ltab, ssm_d, l)
        ys = _glu_call(y_pre, w_glu_b, l)
        merged = _merge_call(ys, yc, w_brs_b, w_brc_b, l, gates)
        xt = _resid_call(merged, w_o_b, l, xt, mod, g1, 1024, 512, "out_proj")

        h2, info, infot, k16 = _router_call(xt, norm2_g, mod, sc2, sh2, w_r, b_r, l)
        dstb, nbands, dstp, npad, cexp, n_used = _moe_plan(k16, n_chunks)
        xs = _dispatch_call(h2, infot, dstb, nbands, dstp, npad, n_rows)
        ye = _experts_call(xs, wg_b, wu_b, wd_b, l, cexp, n_used)
        xt = _combine_call(ye, info, xt, mod, l, g2, dstb, nbands)

    out = _norm_call(xt, final_g[None, :])
    return out.reshape(bsz, seq, d)
```

```python
import functools
import math

import jax
import jax.numpy as jnp
from jax import lax
from jax.experimental import pallas as pl
from jax.experimental.pallas import tpu as pltpu

F32 = jnp.float32
BF16 = jnp.bfloat16
I32 = jnp.int32

EPS = 1e-6
SSM_GROUP = 16
SSM_STATE = 64
CONV_WIDTH = 3
N_GROUPS = 4
EXPERTS_PER_GROUP = 4
N_EXPERTS = N_GROUPS * EXPERTS_PER_GROUP
N_MOD = 6

LANES = 128
SUBLANES = 8
VMEM_LIMIT_BYTES = 58 * 1024 * 1024

SSM_LC = 16
SSM_LA = 4
SSM_LB = 4
SSM_S2 = 16
SSM_K2 = 16
SSM_NK = SSM_S2 * SSM_K2
SSM_TT = SSM_NK * SSM_LC
SSM_PITCH = SSM_S2 * SSM_LC + SUBLANES
SSM_CB = 256
SSM_GB = SSM_CB // SSM_GROUP
SSM_NS = SSM_GB * SSM_STATE
LT_ROWS = 24
SSM_NPOW = 32

MOE_TM = 512
BAND = 2 * SUBLANES
MOE_MAXR = 2 * MOE_TM + N_EXPERTS * BAND
MOE_MAXB = MOE_MAXR // BAND
MOE_TMG = 512
MOE_COLS = 1024


def _cparams(sem):
    return pltpu.CompilerParams(dimension_semantics=sem, vmem_limit_bytes=VMEM_LIMIT_BYTES)


def _mod_kernel(c_ref, w_ref, t_ref, o_ref):
    c = c_ref[...]
    s = c * jax.nn.sigmoid(c)
    shared = jnp.sum(s * w_ref[...], axis=0, keepdims=True)
    o_ref[...] = t_ref[...] + shared


def _mod_call(c, w_mod, mod_table):
    d, n = w_mod.shape
    depth = mod_table.shape[0]
    tn = 512
    return pl.pallas_call(
        _mod_kernel,
        grid=(n // tn,),
        in_specs=[pl.BlockSpec((d, 1), lambda j: (0, 0)),
                  pl.BlockSpec((d, tn), lambda j: (0, j)),
                  pl.BlockSpec((depth, tn), lambda j: (0, j))],
        out_specs=pl.BlockSpec((depth, tn), lambda j: (0, j)),
        out_shape=jax.ShapeDtypeStruct((depth, n), F32),
        compiler_params=_cparams(("arbitrary",)),
        name="adaln_mod",
    )(c.reshape(d, 1), w_mod, mod_table)


def _norm_mod_kernel(x_ref, g_ref, sc_ref, sh_ref, o_ref):
    x = x_ref[...]
    y = x * lax.rsqrt(jnp.mean(x * x, axis=-1, keepdims=True) + EPS)
    o_ref[...] = ((y * g_ref[...]) * (1.0 + sc_ref[...]) + sh_ref[...]).astype(o_ref.dtype)


def _norm_kernel(x_ref, g_ref, o_ref):
    x = x_ref[...]
    y = x * lax.rsqrt(jnp.mean(x * x, axis=-1, keepdims=True) + EPS)
    o_ref[...] = (y * g_ref[...]).astype(o_ref.dtype)


def _norm_mod_call(x, gains, mod, l, sc_blk, sh_blk, tm=256):
    t, d = x.shape
    return pl.pallas_call(
        _norm_mod_kernel,
        grid=(t // tm,),
        in_specs=[pl.BlockSpec((tm, d), lambda i: (i, 0)),
                  pl.BlockSpec((None, 1, d), lambda i: (l, 0, 0)),
                  pl.BlockSpec((None, 1, d), lambda i: (l, 0, sc_blk)),
                  pl.BlockSpec((None, 1, d), lambda i: (l, 0, sh_blk))],
        out_specs=pl.BlockSpec((tm, d), lambda i: (i, 0)),
        out_shape=jax.ShapeDtypeStruct((t, d), BF16),
        compiler_params=_cparams(("parallel",)),
        name="norm_mod",
    )(x, gains, mod, mod)


def _norm_call(x, g, tm=256):
    t, d = x.shape
    return pl.pallas_call(
        _norm_kernel,
        grid=(t // tm,),
        in_specs=[pl.BlockSpec((tm, d), lambda i: (i, 0)), pl.BlockSpec((1, d), lambda i: (0, 0))],
        out_specs=pl.BlockSpec((tm, d), lambda i: (i, 0)),
        out_shape=jax.ShapeDtypeStruct((t, d), F32),
        compiler_params=_cparams(("parallel",)),
        name="final_norm",
    )(x, g)


def _wspec(l, k, tn):
    return pl.BlockSpec((None, k, tn), lambda i, j: (l, 0, j))


def _mm_kernel(a_ref, w_ref, o_ref):
    o_ref[...] = jnp.dot(a_ref[...], w_ref[...], preferred_element_type=F32).astype(o_ref.dtype)


def _mm_call(a, w, l, col0, n, out_dtype, tm, tn, name):
    t, k = a.shape
    cb = col0 // tn
    return pl.pallas_call(
        _mm_kernel,
        grid=(t // tm, n // tn),
        in_specs=[pl.BlockSpec((tm, k), lambda i, j: (i, 0)),
                  pl.BlockSpec((None, k, tn), lambda i, j: (l, 0, cb + j))],
        out_specs=pl.BlockSpec((tm, tn), lambda i, j: (i, j)),
        out_shape=jax.ShapeDtypeStruct((t, n), out_dtype),
        compiler_params=_cparams(("parallel", "arbitrary")),
        name=name,
    )(a, w)


def _glu_kernel(a_ref, y_ref, w_ref, o_ref):
    acc = jnp.dot(a_ref[...], w_ref[...], preferred_element_type=F32)
    o_ref[...] = (y_ref[...].astype(F32) * jax.nn.sigmoid(acc)).astype(o_ref.dtype)


def _glu_call(y, w, l, tm=1024, tn=512):
    t, k = y.shape
    n = w.shape[2]
    return pl.pallas_call(
        _glu_kernel,
        grid=(t // tm, n // tn),
        in_specs=[pl.BlockSpec((tm, k), lambda i, j: (i, 0)),
                  pl.BlockSpec((tm, tn), lambda i, j: (i, j)),
                  _wspec(l, k, tn)],
        out_specs=pl.BlockSpec((tm, tn), lambda i, j: (i, j)),
        out_shape=jax.ShapeDtypeStruct((t, n), BF16),
        compiler_params=_cparams(("parallel", "arbitrary")),
        name="s5_glu",
    )(y, y, w)


def _merge_kernel(ys_ref, yc_ref, ws_ref, wc_ref, gs_ref, gc_ref, o_ref):
    y_ssm = jnp.dot(ys_ref[...], ws_ref[...], preferred_element_type=F32)
    y_conv = jnp.dot(yc_ref[...], wc_ref[...], preferred_element_type=F32)
    merged = (jax.nn.sigmoid(gs_ref[...].astype(F32)) * y_ssm
              + jax.nn.sigmoid(gc_ref[...].astype(F32)) * y_conv)
    o_ref[...] = merged.astype(o_ref.dtype)


def _merge_call(ys, yc, ws, wc, l, gates, tm=1024, tn=512):
    t, k = ys.shape
    n = ws.shape[2]
    gs_blk, gc_blk = 0, n // tn
    return pl.pallas_call(
        _merge_kernel,
        grid=(t // tm, n // tn),
        in_specs=[pl.BlockSpec((tm, k), lambda i, j: (i, 0)),
                  pl.BlockSpec((tm, k), lambda i, j: (i, 0)),
                  _wspec(l, k, tn),
                  _wspec(l, k, tn),
                  pl.BlockSpec((tm, tn), lambda i, j: (i, gs_blk + j)),
                  pl.BlockSpec((tm, tn), lambda i, j: (i, gc_blk + j))],
        out_specs=pl.BlockSpec((tm, tn), lambda i, j: (i, j)),
        out_shape=jax.ShapeDtypeStruct((t, n), BF16),
        compiler_params=_cparams(("parallel", "arbitrary")),
        name="branch_merge",
    )(ys, yc, ws, wc, gates, gates)


def _resid_kernel(a_ref, w_ref, x_ref, g_ref, o_ref):
    acc = jnp.dot(a_ref[...], w_ref[...], preferred_element_type=F32)
    o_ref[...] = x_ref[...] + g_ref[...] * acc


def _resid_call(a, w, l, x, mod, gate_blk, tm, tn, name):
    t, k = a.shape
    n = w.shape[2]
    gb = gate_blk * (n // tn)
    return pl.pallas_call(
        _resid_kernel,
        grid=(t // tm, n // tn),
        in_specs=[pl.BlockSpec((tm, k), lambda i, j: (i, 0)),
                  _wspec(l, k, tn),
                  pl.BlockSpec((tm, tn), lambda i, j: (i, j)),
                  pl.BlockSpec((None, 1, tn), lambda i, j: (l, 0, gb + j))],
        out_specs=pl.BlockSpec((tm, tn), lambda i, j: (i, j)),
        out_shape=jax.ShapeDtypeStruct((t, n), F32),
        compiler_params=_cparams(("parallel", "arbitrary")),
        name=name,
    )(a, w, x, mod)


def _proj_conv_kernel(a_ref, wb_ref, wc_ref, wv_ref, w_ref, o_ref, z_scr, carry):
    tm = a_ref.shape[0]
    halo = SUBLANES
    i, k = pl.program_id(0), pl.program_id(1)

    @pl.when((i == 0) & (k == 0))
    def _():
        carry[...] = jnp.zeros_like(carry)

    a = a_ref[...]
    bg = jnp.dot(a, wb_ref[...], preferred_element_type=F32)
    z = (jnp.dot(a, wc_ref[...], preferred_element_type=F32)
         * jnp.dot(a, wv_ref[...], preferred_element_type=F32))
    z_scr[0:halo, :] = carry[k]
    z_scr[halo:, :] = z
    carry[k] = z[tm - halo:, :]
    w = w_ref[...]
    conv = w[2:3, :] * z
    for tap in range(CONV_WIDTH - 1):
        conv = conv + w[tap:tap + 1, :] * z_scr[pl.ds(halo - (CONV_WIDTH - 1 - tap), tm), :]
    o_ref[...] = (bg * conv).astype(o_ref.dtype)


def _proj_conv_call(a, w, l, conv_w, b_col, c_col, v_col, width, tm=1024, tc=256):
    t, kdim = a.shape
    bb, cb, vb = b_col // tc, c_col // tc, v_col // tc

    def wspec(off):
        return pl.BlockSpec((None, kdim, tc), lambda i, k: (l, 0, off + k))

    return pl.pallas_call(
        _proj_conv_kernel,
        grid=(t // tm, width // tc),
        in_specs=[pl.BlockSpec((tm, kdim), lambda i, k: (i, 0)),
                  wspec(bb), wspec(cb), wspec(vb),
                  pl.BlockSpec((None, CONV_WIDTH, tc), lambda i, k: (l, 0, k))],
        out_specs=pl.BlockSpec((tm, tc), lambda i, k: (i, k)),
        out_shape=jax.ShapeDtypeStruct((t, width), BF16),
        scratch_shapes=[pltpu.VMEM((tm + SUBLANES, tc), F32),
                        pltpu.VMEM((width // tc, SUBLANES, tc), F32)],
        compiler_params=_cparams(("arbitrary", "arbitrary")),
        name="in_proj_conv",
    )(a, w, w, w, conv_w)


def _cmul_row(xr, xi, lr, li):
    return xr * lr - xi * li, xr * li + xi * lr


def _gelu_tanh(x):
    return 0.5 * x * (1.0 + jnp.tanh(math.sqrt(2.0 / math.pi) * (x + 0.044715 * (x * x * x))))


def _expand_block_diag(dst, spread_rows, row_shift, col_shift, chunk):
    n_r, n_c = dst.shape
    gmask = SSM_GB - 1
    for r0 in range(0, n_r, chunk):
        spread = spread_rows(r0, chunk)
        rg = (lax.broadcasted_iota(I32, (chunk, n_c), 0) + r0) >> row_shift
        cg = lax.broadcasted_iota(I32, (chunk, n_c), 1) >> col_shift
        keep = (rg & gmask) == (cg & gmask)
        dst[r0:r0 + chunk, :] = jnp.where(keep, spread, 0.0).astype(dst.dtype)


def _ssm_build_tables(pc_ref, qc_ref, kc_ref, p_scr, q_scr, k_scr):
    ns2 = 2 * SSM_NS
    p_state = SSM_STATE
    lp, lc = int(math.log2(p_state)), int(math.log2(SSM_GROUP))
    lns = int(math.log2(SSM_NS))
    r = lax.broadcasted_iota(I32, (LANES, ns2), 0)
    c = lax.broadcasted_iota(I32, (LANES, ns2), 1)
    tile_p = jnp.where(((r >> lp) == (c >> lns)) & ((r & (p_state - 1)) == (c & (p_state - 1))),
                       1.0, 0.0).astype(BF16)

    def p_rows(r0, chunk):
        return jnp.dot(pc_ref[r0:r0 + chunk, :], tile_p, preferred_element_type=F32)

    _expand_block_diag(p_scr, p_rows, lc, lp, 256)
    r = lax.broadcasted_iota(I32, (ns2, LANES), 0)
    c = lax.broadcasted_iota(I32, (ns2, LANES), 1)
    tile_q = jnp.where(((c >> lp) == (r >> lns)) & ((c & (p_state - 1)) == (r & (p_state - 1))),
                       1.0, 0.0).astype(BF16)

    def q_rows(r0, chunk):
        return jnp.dot(tile_q[r0:r0 + chunk, :], qc_ref[...], preferred_element_type=F32)

    _expand_block_diag(q_scr, q_rows, lp, lc, 512)
    r = lax.broadcasted_iota(I32, (LANES, SSM_CB), 0)
    c = lax.broadcasted_iota(I32, (LANES, SSM_CB), 1)
    tile_k = jnp.where(r == (c & (SSM_GROUP - 1)), 1.0, 0.0).astype(BF16)

    def k_rows(r0, chunk):
        return jnp.dot(kc_ref[r0:r0 + chunk, :], tile_k, preferred_element_type=F32)

    _expand_block_diag(k_scr, k_rows, lc, lc, 1024)


def _ssm_kernel(u0_ref, u1_ref, kc_ref, pc_ref, qc_ref, lt_ref, d_ref, o_ref,
                ub2, y2, bx, s_scr, ha, cs, carry, yo0, yo1, kall_ref, p_ref, q_ref, up0, up1):
    ns = SSM_NS
    cb = SSM_CB
    rows = SSM_K2
    ksteps = SSM_S2 * SSM_LC
    kstride = SSM_PITCH
    u_halves = (up0, up1)
    yo = (yo0, yo1)

    for src, dst in ((u0_ref, up0), (u1_ref, up1)):
        for k2 in range(SSM_K2):
            dst[k2 * kstride:k2 * kstride + ksteps, :] = src[k2 * ksteps:(k2 + 1) * ksteps, :]

    @pl.when(pl.program_id(1) == 0)
    def _():
        carry[...] = jnp.zeros_like(carry)
        _ssm_build_tables(pc_ref, qc_ref, kc_ref, p_ref, q_ref, kall_ref)

    def lt(row):
        return lt_ref[row:row + 1, 0:ns], lt_ref[row:row + 1, ns:2 * ns]

    for s in range(SSM_LC):
        for s2 in range(SSM_S2):
            idx = pl.ds(s2 * SSM_LC + s, rows, stride=kstride)
            for hf, u_ref in enumerate(u_halves):
                c0 = s * cb + hf * LANES
                ub2[s2 * rows:(s2 + 1) * rows, c0:c0 + LANES] = u_ref[idx, :].astype(BF16)

    for sp in range(SSM_LC):
        y2[:, sp * cb:(sp + 1) * cb] = jnp.dot(
            ub2[:, 0:(sp + 1) * cb], kall_ref[(SSM_LC - 1 - sp) * cb:, :],
            preferred_element_type=F32)

    s_scr[...] = jnp.dot(ub2[:, (SSM_LA - 1) * SSM_LB * cb:], p_ref[...], preferred_element_type=F32)
    for a in range(SSM_LA - 1):
        bx[...] = jnp.dot(ub2[:, a * SSM_LB * cb:(a + 1) * SSM_LB * cb], p_ref[...],
                          preferred_element_type=F32)
        lr, li = lt(SSM_LA - 2 - a)
        for s2 in range(SSM_S2):
            sl = slice(s2 * rows, (s2 + 1) * rows)
            pr, pi = _cmul_row(bx[sl, 0:ns], bx[sl, ns:], lr, li)
            s_scr[sl, 0:ns] += pr
            s_scr[sl, ns:] += pi

    lr, li = lt(4 + 1)
    for s2 in range(1, SSM_S2):
        pv = slice((s2 - 1) * rows, s2 * rows)
        sl = slice(s2 * rows, (s2 + 1) * rows)
        pr, pi = _cmul_row(s_scr[pv, 0:ns], s_scr[pv, ns:], lr, li)
        s_scr[sl, 0:ns] += pr
        s_scr[sl, ns:] += pi

    lr, li = lt(3)
    last = slice((SSM_S2 - 1) * rows, SSM_S2 * rows)
    zr, zi = s_scr[last, 0:ns], s_scr[last, ns:]
    cr, ci = carry[0:1, 0:ns], carry[0:1, ns:]
    for k2 in range(SSM_K2):
        cs[k2:k2 + 1, 0:ns] = cr
        cs[k2:k2 + 1, ns:] = ci
        pr, pi = _cmul_row(cr, ci, lr, li)
        cr, ci = pr + zr[k2:k2 + 1, :], pi + zi[k2:k2 + 1, :]
    carry[0:1, 0:ns] = cr
    carry[0:1, ns:] = ci

    for a in range(SSM_LA):
        for s2 in range(SSM_S2):
            sl = slice(s2 * rows, (s2 + 1) * rows)
            hr, hi = cs[:, 0:ns], cs[:, ns:]
            if s2 > 0:
                pv = slice((s2 - 1) * rows, s2 * rows)
                lr, li = lt(4 + s2)
                pr, pi = _cmul_row(hr, hi, lr, li)
                hr, hi = s_scr[pv, 0:ns] + pr, s_scr[pv, ns:] + pi
            if a > 0:
                lr, li = lt(a - 1)
                hr, hi = _cmul_row(hr, hi, lr, li)
            ha[sl, 0:ns] = hr.astype(BF16)
            ha[sl, ns:] = hi.astype(BF16)
        wa = SSM_LB * cb
        y2[:, a * wa:(a + 1) * wa] += jnp.dot(ha[...], q_ref[...], preferred_element_type=F32)

    d = d_ref[...]
    for s in range(SSM_LC):
        for s2 in range(SSM_S2):
            idx = pl.ds(s2 * SSM_LC + s, rows, stride=kstride)
            for hf, u_ref in enumerate(u_halves):
                c0 = s * cb + hf * LANES
                y = (y2[s2 * rows:(s2 + 1) * rows, c0:c0 + LANES]
                     + d[:, hf * LANES:(hf + 1) * LANES] * u_ref[idx, :])
                yo[hf][idx, :] = _gelu_tanh(y)
    for hf in range(len(u_halves)):
        for k2 in range(SSM_K2):
            o_ref[k2 * ksteps:(k2 + 1) * ksteps, hf * LANES:(hf + 1) * LANES] = (
                yo[hf][k2 * kstride:k2 * kstride + ksteps, :].astype(o_ref.dtype))


def _ssm_call(proj, u_col, h_ssm, kc, pc, qc, ltab, d_skip, l):
    t = proj.shape[0]
    nb = h_ssm // SSM_CB
    ub = u_col // SSM_CB
    ns2 = 2 * SSM_NS

    def tab(r, c):
        return pl.BlockSpec((None, None, r, c), lambda b, i: (l, b, 0, 0))

    return pl.pallas_call(
        _ssm_kernel,
        grid=(nb, t // SSM_TT),
        in_specs=[pl.BlockSpec((SSM_TT, LANES), lambda b, i: (i, 2 * (ub + b))),
                  pl.BlockSpec((SSM_TT, LANES), lambda b, i: (i, 2 * (ub + b) + 1)),
                  tab(SSM_LC * SSM_CB, LANES),
                  tab(SSM_LB * SSM_CB, LANES),
                  tab(LANES, SSM_LB * SSM_CB),
                  tab(LT_ROWS, ns2),
                  pl.BlockSpec((None, 1, SSM_CB), lambda b, i: (l, 0, b))],
        out_specs=pl.BlockSpec((SSM_TT, SSM_CB), lambda b, i: (i, b)),
        out_shape=jax.ShapeDtypeStruct((t, h_ssm), BF16),
        scratch_shapes=[pltpu.VMEM((SSM_NK, SSM_LC * SSM_CB), BF16),
                        pltpu.VMEM((SSM_NK, SSM_LC * SSM_CB), F32),
                        pltpu.VMEM((SSM_NK, ns2), F32),
                        pltpu.VMEM((SSM_NK, ns2), F32),
                        pltpu.VMEM((SSM_NK, ns2), BF16),
                        pltpu.VMEM((SSM_K2, ns2), F32),
                        pltpu.VMEM((SUBLANES, ns2), F32),
                        pltpu.VMEM((SSM_K2 * SSM_PITCH, LANES), F32),
                        pltpu.VMEM((SSM_K2 * SSM_PITCH, LANES), F32),
                        pltpu.VMEM((SSM_LC * SSM_CB, SSM_CB), BF16),
                        pltpu.VMEM((SSM_LB * SSM_CB, ns2), BF16),
                        pltpu.VMEM((ns2, SSM_LB * SSM_CB), BF16),
                        pltpu.VMEM((SSM_K2 * SSM_PITCH, LANES), F32),
                        pltpu.VMEM((SSM_K2 * SSM_PITCH, LANES), F32)],
        compiler_params=_cparams(("arbitrary", "arbitrary")),
        name="s5_scan",
    )(proj, proj, kc, pc, qc, ltab, d_skip)


def _ssm_tables(a_re, a_im, log_dt, b_re, b_im, c_re, c_im):
    assert 2 * a_re.shape[2] == LANES
    nl, g, p = a_re.shape
    gc = SSM_GROUP
    gb = SSM_GB
    nb = g // gb
    hi = lax.Precision.HIGHEST
    delta = jnp.exp(log_dt)[..., None]
    mag = jnp.exp(delta * a_re)
    ang = delta * a_im
    lam = (mag * jnp.cos(ang), mag * jnp.sin(ang))
    den = a_re * a_re + a_im * a_im
    zr, zi = lam[0] - 1.0, lam[1]
    coef_re = (zr * a_re + zi * a_im) / den
    coef_im = (zi * a_re - zr * a_im) / den
    bt_re = coef_re[..., None] * b_re - coef_im[..., None] * b_im
    bt_im = coef_re[..., None] * b_im + coef_im[..., None] * b_re

    def cmul(x, y):
        return x[0] * y[0] - x[1] * y[1], x[0] * y[1] + x[1] * y[0]

    def power_table(base):
        tab = (jnp.stack([jnp.ones_like(base[0]), base[0]]), jnp.stack([jnp.zeros_like(base[1]), base[1]]))
        step = cmul(base, base)
        while tab[0].shape[0] < SSM_NPOW:
            nxt = cmul(tab, (step[0][None], step[1][None]))
            tab = (jnp.concatenate([tab[0], nxt[0]]), jnp.concatenate([tab[1], nxt[1]]))
            step = cmul(step, step)
        return tab

    pw = power_table(lam)
    pw16 = power_table((pw[0][SSM_LC], pw[1][SSM_LC]))

    wr = pw[0][:SSM_LC, ..., None] * bt_re[None] - pw[1][:SSM_LC, ..., None] * bt_im[None]
    wi = pw[0][:SSM_LC, ..., None] * bt_im[None] + pw[1][:SSM_LC, ..., None] * bt_re[None]

    k = (jnp.einsum("lgop,tlgpi->tlgio", c_re, wr, precision=hi)
         - jnp.einsum("lgop,tlgpi->tlgio", c_im, wi, precision=hi))
    k = k[::-1].reshape(SSM_LC, nl, nb, gb, gc, gc).transpose(1, 2, 0, 3, 4, 5)
    kc = k.reshape(nl, nb, SSM_LC * SSM_CB, gc)
    kc = jnp.pad(kc, ((0, 0), (0, 0), (0, 0), (0, LANES - gc))).astype(BF16)

    def p_rows(w):
        return w[SSM_LB - 1::-1].reshape(SSM_LB, nl, nb, gb, p, gc).transpose(1, 2, 0, 3, 5, 4)

    pc = jnp.stack([p_rows(wr), p_rows(wi)], axis=5)
    pc = pc.reshape(nl, nb, SSM_LB * SSM_CB, 2 * p).astype(BF16)

    qw = (pw[0][1:SSM_LB + 1, :, :, None, :], pw[1][1:SSM_LB + 1, :, :, None, :])
    dr = c_re[None] * qw[0] - c_im[None] * qw[1]
    di = c_re[None] * qw[1] + c_im[None] * qw[0]

    def q_cols(w):
        return w.reshape(SSM_LB, nl, nb, gb, gc, p).transpose(1, 2, 5, 0, 3, 4)

    qc = jnp.stack([q_cols(dr), q_cols(-di)], axis=2)
    qc = qc.reshape(nl, nb, 2 * p, SSM_LB * SSM_CB).astype(BF16)

    def lrows(part):
        r = jnp.concatenate([pw[part][SSM_LB:SSM_LC:SSM_LB], pw16[part][SSM_S2:SSM_S2 + 1],
                             pw16[part][:SSM_S2]])
        r = jnp.concatenate([r, jnp.zeros((LT_ROWS - r.shape[0],) + r.shape[1:], F32)])
        return r.reshape(LT_ROWS, nl, nb, SSM_NS).transpose(1, 2, 0, 3)

    ltab = jnp.concatenate([lrows(0), lrows(1)], axis=3)
    return kc, pc, qc, ltab


def _router_kernel(x_ref, gain_ref, sc_ref, sh_ref, w_ref, b_ref, h_ref, info_ref, infot_ref, k16_ref):
    tm = x_ref.shape[0]
    x = x_ref[...]
    y = x * lax.rsqrt(jnp.mean(x * x, axis=-1, keepdims=True) + EPS)
    h = ((y * gain_ref[...]) * (1.0 + sc_ref[...]) + sh_ref[...]).astype(h_ref.dtype)
    h_ref[...] = h
    lg = jnp.dot(h, w_ref[...], preferred_element_type=F32) + b_ref[...]
    col = lax.broadcasted_iota(I32, lg.shape, 1)
    big = jnp.int32(LANES)
    neg = -jnp.inf
    is_g = col < N_GROUPS
    lgg = jnp.where(is_g, lg, neg)
    gmax = jnp.max(lgg, axis=1, keepdims=True)
    gsel = jnp.min(jnp.where(is_g & (lgg == gmax), col, big), axis=1, keepdims=True)
    denom = jnp.sum(jnp.where(is_g, jnp.exp(lgg - gmax), 0.0), axis=1, keepdims=True)
    p_sel = 1.0 / denom
    ecol = col - N_GROUPS
    egrp = lax.shift_right_arithmetic(ecol, jnp.int32(int(math.log2(EXPERTS_PER_GROUP))))
    in_grp = (ecol >= 0) & (ecol < N_EXPERTS) & (egrp == gsel)
    le = jnp.where(in_grp, lg, neg)
    m1 = jnp.max(le, axis=1, keepdims=True)
    i1 = jnp.min(jnp.where(in_grp & (le == m1), col, big), axis=1, keepdims=True)
    rest = in_grp & (col != i1)
    le2 = jnp.where(rest, lg, neg)
    m2 = jnp.max(le2, axis=1, keepdims=True)
    i2 = jnp.min(jnp.where(rest & (le2 == m2), col, big), axis=1, keepdims=True)
    e2x = jnp.exp(m2 - m1)
    w1 = p_sel / (1.0 + e2x)
    w2 = p_sel * e2x / (1.0 + e2x)

    oh1 = col == (i1 - N_GROUPS)
    oh2 = col == (i2 - N_GROUPS)
    oh = jnp.where(oh1 | oh2, 1.0, 0.0)
    r_i = lax.broadcasted_iota(I32, (tm, tm), 0)
    c_i = lax.broadcasted_iota(I32, (tm, tm), 1)
    earlier = jnp.where(r_i > c_i, 1.0, 0.0).astype(BF16)
    before = jnp.dot(earlier, oh.astype(BF16), preferred_element_type=F32)
    cnt = jnp.sum(oh, axis=0, keepdims=True)
    k16 = jnp.floor((cnt + (BAND - 1.0)) * (1.0 / BAND))
    k16_8 = jnp.broadcast_to(k16, (SUBLANES, LANES))
    e_r = lax.broadcasted_iota(I32, (LANES, LANES), 0)
    e_c = lax.broadcasted_iota(I32, (LANES, LANES), 1)
    lower_e = jnp.where(e_r < e_c, 1.0, 0.0).astype(BF16)
    seg_start = jnp.dot(k16_8.astype(BF16), lower_e, preferred_element_type=F32)[0:1, :] * BAND
    base = before + seg_start
    lp1 = jnp.sum(jnp.where(oh1, base, 0.0), axis=1, keepdims=True)
    lp2 = jnp.sum(jnp.where(oh2, base, 0.0), axis=1, keepdims=True)

    info = jnp.where(col == 0, lp1, jnp.where(col == 1, lp2, jnp.where(col == 2, w1, jnp.where(col == 3, w2, 0.0))))
    info_ref[...] = info
    infot_ref[...] = info.T[0:SUBLANES, :]
    k16_ref[...] = k16_8


def _router_call(x, gains, mod, sc_blk, sh_blk, w_r, b_r, l):
    t, d = x.shape
    tm = MOE_TM
    nt = t // tm
    return pl.pallas_call(
        _router_kernel,
        grid=(nt,),
        in_specs=[pl.BlockSpec((tm, d), lambda i: (i, 0)),
                  pl.BlockSpec((None, 1, d), lambda i: (l, 0, 0)),
                  pl.BlockSpec((None, 1, d), lambda i: (l, 0, sc_blk)),
                  pl.BlockSpec((None, 1, d), lambda i: (l, 0, sh_blk)),
                  pl.BlockSpec((None, d, LANES), lambda i: (l, 0, 0)),
                  pl.BlockSpec((None, 1, LANES), lambda i: (l, 0, 0))],
        out_specs=[pl.BlockSpec((tm, d), lambda i: (i, 0)),
                   pl.BlockSpec((tm, LANES), lambda i: (i, 0)),
                   pl.BlockSpec((None, SUBLANES, tm), lambda i: (i, 0, 0)),
                   pl.BlockSpec((None, SUBLANES, LANES), lambda i: (i, 0, 0))],
        out_shape=[jax.ShapeDtypeStruct((t, d), BF16),
                   jax.ShapeDtypeStruct((t, LANES), F32),
                   jax.ShapeDtypeStruct((nt, SUBLANES, tm), F32),
                   jax.ShapeDtypeStruct((nt, SUBLANES, LANES), F32)],
        compiler_params=_cparams(("parallel",)),
        name="moe_router",
    )(x, gains, mod, mod, w_r, b_r)


def _moe_plan(k16, n_chunks):
    k = k16[:, 0, :N_EXPERTS].astype(I32)
    bpc = MOE_TMG // BAND
    tot = jnp.sum(k, axis=0)
    padc = ((tot + bpc - 1) // bpc) * bpc
    end_e = jnp.cumsum(padc)
    start_e = end_e - padc
    gstart = start_e[None, :] + jnp.cumsum(k, axis=0) - k
    lend = jnp.cumsum(k, axis=1)
    lstart = lend - k
    b = jnp.arange(MOE_MAXB, dtype=I32)
    mine = (b[None, :, None] >= lstart[:, None, :]) & (b[None, :, None] < lend[:, None, :])
    dstb = jnp.sum(jnp.where(mine, (gstart - lstart)[:, None, :], 0), axis=2) + b[None, :]
    nbands = lend[:, -1]
    dstb = jnp.where(b[None, :] < nbands[:, None], dstb, 0)
    cstart = jnp.arange(n_chunks, dtype=I32) * bpc
    cexp = jnp.sum((cstart[:, None] >= end_e[None, :]).astype(I32), axis=1)
    cexp = jnp.minimum(cexp, N_EXPERTS - 1)
    n_used = (end_e[-1] // bpc).reshape(1)
    n_tiles = k.shape[0]
    padb = _pad_bands_per_tile(n_tiles)
    total_bands = n_chunks * bpc
    gap_start = jnp.concatenate([start_e + tot, end_e[-1:]])
    gap_len = jnp.concatenate([padc - tot, total_bands - end_e[-1:]])
    pend = jnp.cumsum(gap_len)
    pstart = pend - gap_len
    j = jnp.arange(n_tiles * padb, dtype=I32)
    in_gap = (j[:, None] >= pstart[None, :]) & (j[:, None] < pend[None, :])
    dstp = jnp.sum(jnp.where(in_gap, (gap_start - pstart)[None, :] + j[:, None], 0), axis=1)
    npad = jnp.clip(pend[-1] - jnp.arange(n_tiles, dtype=I32) * padb, 0, padb)
    return dstb.reshape(-1), nbands, dstp, npad, cexp, n_used


def _pad_bands_per_tile(n_tiles):
    worst = (n_tiles * MOE_MAXR + N_EXPERTS * MOE_TMG - 2 * n_tiles * MOE_TM) // BAND
    return -(-worst // n_tiles)


def _band_copy_out(xloc, xs_ref, dstb_ref, sem, tile, b):
    slot = tile % 2
    src = xloc.at[slot, pl.ds(pl.multiple_of(b * BAND, BAND), BAND), :]
    row = pl.multiple_of(dstb_ref[tile * MOE_MAXB + b] * BAND, BAND)
    return pltpu.make_async_copy(src, xs_ref.at[pl.ds(row, BAND), :], sem.at[slot])


def _zero_band_copy(xloc, xs_ref, dstp_ref, sem, padb, tile, j):
    slot = tile % 2
    row = pl.multiple_of(dstp_ref[tile * padb + j] * BAND, BAND)
    return pltpu.make_async_copy(xloc.at[slot, pl.ds(MOE_MAXR, BAND), :],
                                 xs_ref.at[pl.ds(row, BAND), :], sem.at[slot])


def _dispatch_kernel(dstb_ref, nb_ref, dstp_ref, npad_ref, h_ref, lpt_ref, xs_ref, xloc, sem, *, padb):
    tile = pl.program_id(0)
    last = pl.num_programs(0) - 1
    tm, d = h_ref.shape

    def all_copies(t, op):
        def data(b, carry):
            op(_band_copy_out(xloc, xs_ref, dstb_ref, sem, t, b))
            return carry

        def zero(j, carry):
            op(_zero_band_copy(xloc, xs_ref, dstp_ref, sem, padb, t, j))
            return carry

        lax.fori_loop(0, nb_ref[t], data, 0)
        lax.fori_loop(0, npad_ref[t], zero, 0)

    @pl.when(tile >= 2)
    def _():
        all_copies(tile - 2, lambda cp: cp.wait())

    slot = tile % 2
    lp1 = lpt_ref[0:1, :]
    lp2 = lpt_ref[1:2, :]
    rows = lax.broadcasted_iota(I32, (MOE_MAXR + BAND, tm), 0).astype(F32)
    perm = jnp.where((rows == lp1) | (rows == lp2), 1.0, 0.0).astype(BF16)
    for c0 in range(0, d, MOE_COLS):
        xloc[slot, :, c0:c0 + MOE_COLS] = jnp.dot(
            perm, h_ref[:, c0:c0 + MOE_COLS], preferred_element_type=F32).astype(BF16)
    all_copies(tile, lambda cp: cp.start())

    @pl.when(tile == last)
    def _():
        @pl.when(tile >= 1)
        def _():
            all_copies(tile - 1, lambda cp: cp.wait())
        all_copies(tile, lambda cp: cp.wait())


def _dispatch_call(h, infot, dstb, nbands, dstp, npad, n_rows):
    t, d = h.shape
    nt = t // MOE_TM
    return pl.pallas_call(
        functools.partial(_dispatch_kernel, padb=_pad_bands_per_tile(nt)),
        grid_spec=pltpu.PrefetchScalarGridSpec(
            num_scalar_prefetch=4,
            grid=(nt,),
            in_specs=[pl.BlockSpec((MOE_TM, d), lambda i, *_: (i, 0)),
                      pl.BlockSpec((None, SUBLANES, MOE_TM), lambda i, *_: (i, 0, 0))],
            out_specs=pl.BlockSpec(memory_space=pl.ANY),
            scratch_shapes=[pltpu.VMEM((2, MOE_MAXR + BAND, d), BF16), pltpu.SemaphoreType.DMA((2,))]),
        out_shape=jax.ShapeDtypeStruct((n_rows, d), BF16),
        compiler_params=_cparams(("arbitrary",)),
        name="moe_dispatch",
    )(dstb, nbands, dstp, npad, h, infot)


def _experts_kernel(cexp_ref, nused_ref, x_ref, wg_ref, wu_ref, wd_ref, y_ref, wgu_ref):
    f = wd_ref.shape[0]
    c = pl.program_id(0)
    prev = cexp_ref[jnp.maximum(c - 1, 0)]

    @pl.when((c == 0) | (cexp_ref[c] != prev))
    def _():
        wgu_ref[:, 0:f] = wg_ref[...]
        wgu_ref[:, f:] = wu_ref[...]

    @pl.when(c < nused_ref[0])
    def _():
        half = x_ref.shape[0] // 2
        parts = [slice(0, half), slice(half, 2 * half)]
        gus = [jnp.dot(x_ref[rows, :], wgu_ref[...], preferred_element_type=F32) for rows in parts]
        for rows, gu in zip(parts, gus):
            g, u = gu[:, 0:f], gu[:, f:]
            act = ((g * jax.nn.sigmoid(g)) * u).astype(BF16)
            y_ref[rows, :] = jnp.dot(act, wd_ref[...], preferred_element_type=F32).astype(y_ref.dtype)

    @pl.when(c >= nused_ref[0])
    def _():
        y_ref[...] = jnp.zeros_like(y_ref)


def _experts_call(xs, wg, wu, wd, l, cexp, n_used):
    n_rows, d = xs.shape
    f = wd.shape[2]
    n_chunks = n_rows // MOE_TMG
    return pl.pallas_call(
        _experts_kernel,
        grid_spec=pltpu.PrefetchScalarGridSpec(
            num_scalar_prefetch=2,
            grid=(n_chunks,),
            in_specs=[pl.BlockSpec((MOE_TMG, d), lambda c, ce, nu: (jnp.minimum(c, nu[0] - 1), 0)),
                      pl.BlockSpec((None, None, d, f), lambda c, ce, nu: (l, ce[c], 0, 0)),
                      pl.BlockSpec((None, None, d, f), lambda c, ce, nu: (l, ce[c], 0, 0)),
                      pl.BlockSpec((None, None, f, d), lambda c, ce, nu: (l, ce[c], 0, 0))],
            out_specs=pl.BlockSpec((MOE_TMG, d), lambda c, ce, nu: (c, 0)),
            scratch_shapes=[pltpu.VMEM((d, 2 * f), BF16)]),
        out_shape=jax.ShapeDtypeStruct((n_rows, d), BF16),
        compiler_params=_cparams(("arbitrary",)),
        name="moe_experts",
    )(cexp, n_used, xs, wg, wu, wd)


def _band_copy_in(y_ref, yloc, dstb_ref, sem, tile, b):
    slot = tile % 2
    row = pl.multiple_of(dstb_ref[tile * MOE_MAXB + b] * BAND, BAND)
    dst = yloc.at[slot, pl.ds(pl.multiple_of(b * BAND, BAND), BAND), :]
    return pltpu.make_async_copy(y_ref.at[pl.ds(row, BAND), :], dst, sem.at[slot])


def _combine_kernel(dstb_ref, nb_ref, y_ref, info_ref, x_ref, g_ref, o_ref, yloc, sem):
    tile = pl.program_id(0)
    last = pl.num_programs(0) - 1
    tm, d = x_ref.shape

    def fetch(t):
        def start(b, carry):
            _band_copy_in(y_ref, yloc, dstb_ref, sem, t, b).start()
            return carry

        def clear(b, carry):
            yloc[t % 2, pl.ds(pl.multiple_of(b * BAND, BAND), BAND), :] = jnp.zeros((BAND, d), yloc.dtype)
            return carry

        lax.fori_loop(0, nb_ref[t], start, 0)
        lax.fori_loop(nb_ref[t], MOE_MAXB, clear, 0)

    @pl.when(tile == 0)
    def _():
        fetch(tile)

    @pl.when(tile < last)
    def _():
        fetch(tile + 1)

    info = info_ref[...]
    lp1, lp2, w1, w2 = info[:, 0:1], info[:, 1:2], info[:, 2:3], info[:, 3:4]
    cols = lax.broadcasted_iota(I32, (tm, MOE_MAXR), 1).astype(F32)
    pw = (jnp.where(cols == lp1, w1, 0.0) + jnp.where(cols == lp2, w2, 0.0)).astype(BF16)

    def wait(b, carry):
        _band_copy_in(y_ref, yloc, dstb_ref, sem, tile, b).wait()
        return carry

    lax.fori_loop(0, nb_ref[tile], wait, 0)
    slot = tile % 2
    for c0 in range(0, d, MOE_COLS):
        sl = slice(c0, c0 + MOE_COLS)
        acc = jnp.dot(pw, yloc[slot, :, sl], preferred_element_type=F32)
        o_ref[:, sl] = x_ref[:, sl] + g_ref[:, sl] * acc


def _combine_call(y, info, x, mod, l, gate_blk, dstb, nbands):
    t, d = x.shape
    nt = t // MOE_TM
    return pl.pallas_call(
        _combine_kernel,
        grid_spec=pltpu.PrefetchScalarGridSpec(
            num_scalar_prefetch=2,
            grid=(nt,),
            in_specs=[pl.BlockSpec(memory_space=pl.ANY),
                      pl.BlockSpec((MOE_TM, LANES), lambda i, *_: (i, 0)),
                      pl.BlockSpec((MOE_TM, d), lambda i, *_: (i, 0)),
                      pl.BlockSpec((None, 1, d), lambda i, *_: (l, 0, gate_blk))],
            out_specs=pl.BlockSpec((MOE_TM, d), lambda i, *_: (i, 0)),
            scratch_shapes=[pltpu.VMEM((2, MOE_MAXR, d), BF16), pltpu.SemaphoreType.DMA((2,))]),
        out_shape=jax.ShapeDtypeStruct((t, d), F32),
        compiler_params=_cparams(("arbitrary",)),
        name="moe_combine",
    )(dstb, nbands, y, info, x, mod)


def kernel(x, c, w_mod, mod_table, norm1_g, w_in, ssm_a_re, ssm_a_im, ssm_log_dt, ssm_b_re, ssm_b_im, ssm_c_re, ssm_c_im, ssm_d, w_glu, w_br_ssm, conv_w, w_br_conv, w_o, norm2_g, w_router_group, b_router_group, w_router_expert, b_router_expert, w_exp_gate, w_exp_up, w_exp_down, final_g):
    bsz, seq, d = x.shape
    depth = mod_table.shape[0]
    h_ssm = ssm_d.shape[1]
    h_conv = conv_w.shape[2]
    n_exp, _, d_expert = w_exp_gate.shape[1:]
    assert bsz == 1 and seq % SSM_TT == 0 and h_ssm % SSM_CB == 0 and seq % MOE_TM == 0
    assert ssm_a_re.shape[1:] == (h_ssm // SSM_GROUP, SSM_STATE) and n_exp == N_EXPERTS
    assert d % MOE_COLS == 0
    u_col, b_col, c_col, v_col = 0, h_ssm, h_ssm + h_conv, h_ssm + 2 * h_conv
    gs_col = h_ssm + 3 * h_conv
    sh1, sc1, g1, sh2, sc2, g2 = range(N_MOD)

    w_in_b = w_in.astype(BF16)
    w_glu_b = w_glu.astype(BF16)
    w_brs_b = w_br_ssm.astype(BF16)
    w_brc_b = w_br_conv.astype(BF16)
    w_o_b = w_o.astype(BF16)
    wg_b = w_exp_gate.astype(BF16)
    wu_b = w_exp_up.astype(BF16)
    wd_b = w_exp_down.astype(BF16)
    pad = LANES - N_GROUPS - N_EXPERTS
    w_r = jnp.concatenate([w_router_group, w_router_expert, jnp.zeros((depth, d, pad), F32)], axis=2).astype(BF16)
    b_r = jnp.concatenate([b_router_group, b_router_expert, jnp.zeros((depth, pad), F32)], axis=1)[:, None, :]
    s5_kc, s5_pc, s5_qc, ltab = _ssm_tables(ssm_a_re, ssm_a_im, ssm_log_dt, ssm_b_re, ssm_b_im,
                                           ssm_c_re, ssm_c_im)

    n_tiles = seq // MOE_TM
    n_rows = n_tiles * MOE_MAXR + N_EXPERTS * MOE_TMG
    n_chunks = n_rows // MOE_TMG

    xt = x.reshape(seq, d)
    mod = _mod_call(c, w_mod, mod_table)[:, None, :]
    norm1_g, norm2_g, ssm_d = norm1_g[:, None, :], norm2_g[:, None, :], ssm_d[:, None, :]

    for l in range(depth):
        h1 = _norm_mod_call(xt, norm1_g, mod, l, sc1, sh1)
        u = _mm_call(h1, w_in_b, l, u_col, h_ssm, F32, 1024, 512, "in_proj_u")
        yc = _proj_conv_call(h1, w_in_b, l, conv_w, b_col, c_col, v_col, h_conv)
        gates = _mm_call(h1, w_in_b, l, gs_col, 2 * d, BF16, 1024, 512, "in_proj_gates")
        y_pre = _ssm_call(u, 0, h_ssm, s5_kc, s5_pc, s5_qc, ltab, ssm_d, l)
        ys = _glu_call(y_pre, w_glu_b, l)
        merged = _merge_call(ys, yc, w_brs_b, w_brc_b, l, gates)
        xt = _resid_call(merged, w_o_b, l, xt, mod, g1, 1024, 512, "out_proj")

        h2, info, infot, k16 = _router_call(xt, norm2_g, mod, sc2, sh2, w_r, b_r, l)
        dstb, nbands, dstp, npad, cexp, n_used = _moe_plan(k16, n_chunks)
        xs = _dispatch_call(h2, infot, dstb, nbands, dstp, npad, n_rows)
        ye = _experts_call(xs, wg_b, wu_b, wd_b, l, cexp, n_used)
        xt = _combine_call(ye, info, xt, mod, l, g2, dstb, nbands)

    out = _norm_call(xt, final_g[None, :])
    return out.reshape(bsz, seq, d)
```
